```python
import jax, jax.numpy as jnp
from jax import lax
import numpy as np

D_MODEL = 1024
BATCH = 16
SEQ = 4096
DEPTH = 4

CHUNK = 64
N_MIXERS = 4
EPS = 1e-6
A_BLOCK = 128
A_HEADS = 8
A_WIDTH = D_MODEL
A_HEAD_DIM = A_WIDTH // A_HEADS
B_WINDOWS = (2, 4, 8, 16)
B_GROUPS = 4
B_WIDTH = D_MODEL
B_GROUP_DIM = B_WIDTH // B_GROUPS
C_WIDTH = D_MODEL
C_CONV = 3
D_WIDTH = D_MODEL
D_CONV = 31
D_FF = 2816
FFN_CONV = 3

kernel_name = 'chunk_causal_hybrid_conv_mlp_trunk'


def _n_layers_of(m):
    return (DEPTH - m + N_MIXERS - 1) // N_MIXERS


def _rmsnorm(x, g):
    xf = x.astype(jnp.float32)
    y = xf * lax.rsqrt(jnp.mean(xf * xf, axis=-1, keepdims=True) + EPS)
    return (y * g.astype(jnp.float32)).astype(x.dtype)


def _layernorm(x, g, b):
    xf = x.astype(jnp.float32)
    mu = jnp.mean(xf, axis=-1, keepdims=True)
    xc = xf - mu
    var = jnp.mean(xc * xc, axis=-1, keepdims=True)
    y = xc * lax.rsqrt(var + EPS) * g.astype(jnp.float32) + b.astype(jnp.float32)
    return y.astype(x.dtype)


def _causal_dwconv(x, w):
    k = w.shape[0]
    return lax.conv_general_dilated(
        x, w[:, None, :].astype(x.dtype), window_strides=(1,), padding=[(k - 1, 0)],
        dimension_numbers=('NWC', 'WIO', 'NWC'), feature_group_count=x.shape[-1])


def _mixer_a(h, w_in, w_s, b_s, ln_g, ln_b, w_out):
    bsz, s, _ = h.shape
    uv = jax.nn.gelu(h @ w_in, approximate=False)
    u, v = jnp.split(uv, 2, axis=-1)
    v = _layernorm(v, ln_g, ln_b)
    cpos = jnp.arange(A_BLOCK) // CHUNK
    mask = (cpos[None, :] <= cpos[:, None]).astype(w_s.dtype)
    w_m = w_s * mask[None]
    v = v.reshape(bsz, s // A_BLOCK, A_BLOCK, A_HEADS, A_HEAD_DIM)
    z = jnp.einsum('gqp,bnpgc->bnqgc', w_m, v) + b_s.T[None, None, :, :, None]
    z = z.reshape(bsz, s, A_WIDTH)
    return (u * z) @ w_out


def _mixer_b(h, w_in, w_grp, scale, w_out):
    bsz, s, _ = h.shape
    p = (h @ w_in).astype(jnp.float32).reshape(bsz, s, B_GROUPS, B_GROUP_DIM)
    cs = jnp.concatenate([jnp.zeros_like(p[:, :1]), jnp.cumsum(p, axis=1)], axis=1)
    t1 = jnp.arange(1, s + 1)
    outs = []
    for g, w in enumerate(B_WINDOWS):
        lo = jnp.maximum(t1 - w, 0)
        cnt = jnp.minimum(t1, w).astype(jnp.float32)
        win = cs[:, 1:, g] - jnp.take(cs[:, :, g], lo, axis=1)
        outs.append(win / cnt[None, :, None] - p[:, :, g])
    pooled = jnp.stack(outs, axis=2).astype(h.dtype)
    mixed = jnp.einsum('bsgc,gcd->bsgd', pooled, w_grp).reshape(bsz, s, B_WIDTH)
    return (mixed * scale) @ w_out


def _mixer_c(h, w_in, conv_w, w_out):
    bg, cg, xv = jnp.split(h @ w_in, 3, axis=-1)
    return (bg * _causal_dwconv(cg * xv, conv_w)) @ w_out


def _mixer_d(h, w1, b1, conv_w, conv_b, ln_g, ln_b, w2, b2):
    a, gate = jnp.split(h @ w1 + b1, 2, axis=-1)
    z = a * jax.nn.sigmoid(gate)
    z = _causal_dwconv(z, conv_w) + conv_b
    z = jax.nn.silu(_layernorm(z, ln_g, ln_b))
    return z @ w2 + b2


def _channel_mixer(h, w_up, conv_w, w_down):
    g, u = jnp.split(_causal_dwconv(h @ w_up, conv_w), 2, axis=-1)
    return (jax.nn.silu(g) * u) @ w_down


def setup_inputs(seed: int = 0) -> dict:
    key = jax.random.key(seed)
    ks = iter(jax.random.split(key, 40))

    def nrm(shape, scale):
        return jax.random.normal(next(ks), shape, jnp.float32) * scale

    def gain(shape):
        return 1.0 + nrm(shape, 0.05)

    na, nb, nc, nd = (_n_layers_of(m) for m in range(N_MIXERS))
    d = D_MODEL
    return {
        'x': nrm((BATCH, SEQ, d), 1.0),
        'norm_mix_pre': gain((DEPTH, d)),
        'norm_mix_post': gain((DEPTH, d)),
        'norm_ffn_pre': gain((DEPTH, d)),
        'norm_ffn_post': gain((DEPTH, d)),
        'a_w_in': nrm((na, d, 2 * A_WIDTH), d ** -0.5),
        'a_w_s': nrm((na, A_HEADS, A_BLOCK, A_BLOCK), A_BLOCK ** -0.5),
        'a_b_s': gain((na, A_HEADS, A_BLOCK)),
        'a_ln_g': gain((na, A_WIDTH)),
        'a_ln_b': nrm((na, A_WIDTH), 0.02),
        'a_w_out': nrm((na, A_WIDTH, d), A_WIDTH ** -0.5),
        'b_w_in': nrm((nb, d, B_WIDTH), d ** -0.5),
        'b_w_grp': nrm((nb, B_GROUPS, B_GROUP_DIM, B_GROUP_DIM), B_GROUP_DIM ** -0.5),
        'b_scale': gain((nb, B_WIDTH)),
        'b_w_out': nrm((nb, B_WIDTH, d), B_WIDTH ** -0.5),
        'c_w_in': nrm((nc, d, 3 * C_WIDTH), d ** -0.5),
        'c_conv_w': nrm((nc, C_CONV, C_WIDTH), C_CONV ** -0.5),
        'c_w_out': nrm((nc, C_WIDTH, d), C_WIDTH ** -0.5),
        'd_w1': nrm((nd, d, 2 * D_WIDTH), d ** -0.5),
        'd_b1': nrm((nd, 2 * D_WIDTH), 0.02),
        'd_conv_w': nrm((nd, D_CONV, D_WIDTH), D_CONV ** -0.5),
        'd_conv_b': nrm((nd, D_WIDTH), 0.02),
        'd_ln_g': gain((nd, D_WIDTH)),
        'd_ln_b': nrm((nd, D_WIDTH), 0.02),
        'd_w2': nrm((nd, D_WIDTH, d), D_WIDTH ** -0.5),
        'd_b2': nrm((nd, d), 0.02),
        'f_w_up': nrm((DEPTH, d, 2 * D_FF), d ** -0.5),
        'f_conv_w': nrm((DEPTH, FFN_CONV, 2 * D_FF), FFN_CONV ** -0.5),
        'f_w_down': nrm((DEPTH, D_FF, d), D_FF ** -0.5),
    }


def reference(x, norm_mix_pre, norm_mix_post, norm_ffn_pre, norm_ffn_post,
              a_w_in, a_w_s, a_b_s, a_ln_g, a_ln_b, a_w_out,
              b_w_in, b_w_grp, b_scale, b_w_out,
              c_w_in, c_conv_w, c_w_out,
              d_w1, d_b1, d_conv_w, d_conv_b, d_ln_g, d_ln_b, d_w2, d_b2,
              f_w_up, f_conv_w, f_w_down):
    for i in range(DEPTH):
        m, j = i % N_MIXERS, i // N_MIXERS
        h = _rmsnorm(x, norm_mix_pre[i])
        if m == 0:
            y = _mixer_a(h, a_w_in[j], a_w_s[j], a_b_s[j], a_ln_g[j], a_ln_b[j], a_w_out[j])
        elif m == 1:
            y = _mixer_b(h, b_w_in[j], b_w_grp[j], b_scale[j], b_w_out[j])
        elif m == 2:
            y = _mixer_c(h, c_w_in[j], c_conv_w[j], c_w_out[j])
        else:
            y = _mixer_d(h, d_w1[j], d_b1[j], d_conv_w[j], d_conv_b[j],
                         d_ln_g[j], d_ln_b[j], d_w2[j], d_b2[j])
        x = x + _rmsnorm(y, norm_mix_post[i])
        h = _rmsnorm(x, norm_ffn_pre[i])
        y = _channel_mixer(h, f_w_up[i], f_conv_w[i], f_w_down[i])
        x = x + _rmsnorm(y, norm_ffn_post[i])
    return x
```

```python
import functools

import jax
import jax.numpy as jnp
from jax import lax
from jax.experimental import pallas as pl
from jax.experimental.pallas import tpu as pltpu

EPS = 1e-6
CHUNK = 64
A_BLOCK = 128
A_HEADS = 8
B_WINDOWS = (2, 4, 8, 16)
B_GROUP_DIM = 256
D_CONV = 31

V7X_VMEM_BYTES = 64 * 2**20
VMEM_LIMIT_BYTES = V7X_VMEM_BYTES - 8 * 2**20
SUBLANES = 8

TM_FFN = 512
FFN_FC = 256
TM_MIX = 512

F32 = jnp.float32
BF16 = jnp.bfloat16


def _rms(x, g):
  ms = jnp.mean(x * x, axis=-1, keepdims=True)
  return x * lax.rsqrt(ms + EPS) * g


def _layernorm(x, g, b):
  mu = jnp.mean(x, axis=-1, keepdims=True)
  xc = x - mu
  var = jnp.mean(xc * xc, axis=-1, keepdims=True)
  return xc * lax.rsqrt(var + EPS) * g + b


def _dot(a, b):
  return jnp.dot(a, b, preferred_element_type=F32)


def _ffn_kernel(x_ref, gpre_ref, gpost_ref, wup_ref, cw_ref, wdn_ref, o_ref,
                h_scr, t_scr, carry_scr, acc_scr, *, tm, fc, nc):
  @pl.when(pl.program_id(1) == 0)
  def _():
    carry_scr[...] = jnp.zeros_like(carry_scr)

  x = x_ref[...]
  h_scr[...] = _rms(x, gpre_ref[...]).astype(BF16)
  acc_scr[...] = jnp.zeros_like(acc_scr)

  def chunk(j, carry):
    t = _dot(h_scr[...], wup_ref[j])
    t_scr[0:SUBLANES, :] = carry_scr[j]
    t_scr[SUBLANES:, :] = t
    carry_scr[j] = t_scr[tm:tm + SUBLANES, :]
    cw = cw_ref[j]
    c = (cw[2:3] * t
         + cw[1:2] * t_scr[SUBLANES - 1:SUBLANES - 1 + tm, :]
         + cw[0:1] * t_scr[SUBLANES - 2:SUBLANES - 2 + tm, :])
    g = c[:, :fc]
    u = c[:, fc:]
    a = (g * jax.nn.sigmoid(g) * u).astype(BF16)
    acc_scr[...] += _dot(a, wdn_ref[j])
    return carry

  lax.fori_loop(0, nc, chunk, 0)
  o_ref[...] = x + _rms(acc_scr[...], gpost_ref[...])


def _ffn_layer(x, g_pre, g_post, w_up, conv_w, w_down):
  bsz, seq, d = x.shape
  d_ff = w_down.shape[0]
  tm, fc = TM_FFN, FFN_FC
  nc = d_ff // fc
  assert nc * fc == d_ff and seq % tm == 0
  wup = (w_up.reshape(d, 2, nc, fc).transpose(2, 0, 1, 3)
         .reshape(nc, d, 2 * fc).astype(BF16))
  cw = conv_w.reshape(3, 2, nc, fc).transpose(2, 0, 1, 3).reshape(nc, 3, 2 * fc)
  wdn = w_down.reshape(nc, fc, d).astype(BF16)
  kern = functools.partial(_ffn_kernel, tm=tm, fc=fc, nc=nc)
  full = lambda shape: pl.BlockSpec(shape, lambda b, s: (0,) * len(shape))
  return pl.pallas_call(
      kern,
      grid=(bsz, seq // tm),
      in_specs=[
          pl.BlockSpec((None, tm, d), lambda b, s: (b, s, 0)),
          full((1, d)), full((1, d)),
          full((nc, d, 2 * fc)), full((nc, 3, 2 * fc)), full((nc, fc, d)),
      ],
      out_specs=pl.BlockSpec((None, tm, d), lambda b, s: (b, s, 0)),
      out_shape=jax.ShapeDtypeStruct(x.shape, x.dtype),
      scratch_shapes=[
          pltpu.VMEM((tm, d), BF16),
          pltpu.VMEM((tm + SUBLANES, 2 * fc), F32),
          pltpu.VMEM((nc, SUBLANES, 2 * fc), F32),
          pltpu.VMEM((tm, d), F32),
      ],
      compiler_params=pltpu.CompilerParams(
          dimension_semantics=("arbitrary", "arbitrary"),
          vmem_limit_bytes=VMEM_LIMIT_BYTES),
      name="ffn",
  )(x, g_pre.reshape(1, d), g_post.reshape(1, d), wup, cw, wdn)


def _mixer_a_kernel(x_ref, gpre_ref, gpost_ref, win_ref, ws_ref, bias_ref,
                    lng_ref, lnb_ref, wout_ref, o_ref, z_scr, *, tm, width):
  x = x_ref[...]
  h = _rms(x, gpre_ref[...]).astype(BF16)
  uv = _dot(h, win_ref[...])
  uv = 0.5 * uv * (1.0 + lax.erf(uv * (2.0 ** -0.5)))
  u = uv[:, :width]
  v = _layernorm(uv[:, width:], lng_ref[...], lnb_ref[...]).astype(BF16)
  q = lax.broadcasted_iota(jnp.int32, (A_BLOCK, A_BLOCK), 0) // CHUNK
  p = lax.broadcasted_iota(jnp.int32, (A_BLOCK, A_BLOCK), 1) // CHUNK
  mask = p <= q
  hd = width // A_HEADS
  for g in range(A_HEADS):
    wm = jnp.where(mask, ws_ref[g], 0.0).astype(BF16)
    for n in range(tm // A_BLOCK):
      rows = slice(n * A_BLOCK, (n + 1) * A_BLOCK)
      cols = slice(g * hd, (g + 1) * hd)
      z_scr[rows, cols] = _dot(wm, v[rows, cols]) + bias_ref[:, cols]
  y = _dot((u * z_scr[...]).astype(BF16), wout_ref[...])
  o_ref[...] = x + _rms(y, gpost_ref[...])


def _mixer_a_layer(x, g_pre, g_post, w_in, w_s, b_s, ln_g, ln_b, w_out):
  bsz, seq, d = x.shape
  width = w_out.shape[0]
  tm = TM_MIX
  hd = width // A_HEADS
  bias = jnp.repeat(b_s.T, hd, axis=1)
  kern = functools.partial(_mixer_a_kernel, tm=tm, width=width)
  full = lambda shape: pl.BlockSpec(shape, lambda b, s: (0,) * len(shape))
  return pl.pallas_call(
      kern,
      grid=(bsz, seq // tm),
      in_specs=[
          pl.BlockSpec((None, tm, d), lambda b, s: (b, s, 0)),
          full((1, d)), full((1, d)),
          full((d, 2 * width)), full((A_HEADS, A_BLOCK, A_BLOCK)),
          full((A_BLOCK, width)), full((1, width)), full((1, width)),
          full((width, d)),
      ],
      out_specs=pl.BlockSpec((None, tm, d), lambda b, s: (b, s, 0)),
      out_shape=jax.ShapeDtypeStruct(x.shape, x.dtype),
      scratch_shapes=[pltpu.VMEM((tm, width), F32)],
      compiler_params=pltpu.CompilerParams(
          dimension_semantics=("arbitrary", "arbitrary"),
          vmem_limit_bytes=VMEM_LIMIT_BYTES),
      name="mixer_a",
  )(x, g_pre.reshape(1, d), g_post.reshape(1, d), w_in.astype(BF16), w_s, bias,
    ln_g.reshape(1, width), ln_b.reshape(1, width), w_out.astype(BF16))


_B_HALO = 16


def _mixer_b_kernel(x_ref, gpre_ref, gpost_ref, win_ref, wgrp_ref, scale_ref,
                    wout_ref, o_ref, ext_scr, mix_scr, *, tm):
  s = pl.program_id(1)

  @pl.when(s == 0)
  def _():
    ext_scr[0:_B_HALO, :] = jnp.zeros((_B_HALO, ext_scr.shape[1]), F32)

  x = x_ref[...]
  h = _rms(x, gpre_ref[...]).astype(BF16)
  ext_scr[_B_HALO:, :] = _dot(h, win_ref[...])
  gd = B_GROUP_DIM
  pos = lax.broadcasted_iota(jnp.int32, (tm, gd), 0) + (s * tm + 1)
  for g, w in enumerate(B_WINDOWS):
    cols = slice(g * gd, (g + 1) * gd)
    p = ext_scr[_B_HALO:, cols]
    win = p
    for k in range(1, w):
      win = win + ext_scr[_B_HALO - k:_B_HALO - k + tm, cols]
    cnt = jnp.minimum(pos, w).astype(F32)
    pooled = (win / cnt - p).astype(BF16)
    mix_scr[:, cols] = _dot(pooled, wgrp_ref[g])
  ext_scr[0:_B_HALO, :] = ext_scr[tm:tm + _B_HALO, :]
  y = _dot((mix_scr[...] * scale_ref[...]).astype(BF16), wout_ref[...])
  o_ref[...] = x + _rms(y, gpost_ref[...])


def _mixer_b_layer(x, g_pre, g_post, w_in, w_grp, scale, w_out):
  bsz, seq, d = x.shape
  width = w_out.shape[0]
  tm = TM_MIX
  ng, gd, _ = w_grp.shape
  assert gd == B_GROUP_DIM and ng == len(B_WINDOWS)
  kern = functools.partial(_mixer_b_kernel, tm=tm)
  full = lambda shape: pl.BlockSpec(shape, lambda b, s: (0,) * len(shape))
  return pl.pallas_call(
      kern,
      grid=(bsz, seq // tm),
      in_specs=[
          pl.BlockSpec((None, tm, d), lambda b, s: (b, s, 0)),
          full((1, d)), full((1, d)),
          full((d, width)), full((ng, gd, gd)), full((1, width)),
          full((width, d)),
      ],
      out_specs=pl.BlockSpec((None, tm, d), lambda b, s: (b, s, 0)),
      out_shape=jax.ShapeDtypeStruct(x.shape, x.dtype),
      scratch_shapes=[pltpu.VMEM((tm + _B_HALO, width), F32),
                      pltpu.VMEM((tm, width), F32)],
      compiler_params=pltpu.CompilerParams(
          dimension_semantics=("arbitrary", "arbitrary"),
          vmem_limit_bytes=VMEM_LIMIT_BYTES),
      name="mixer_b",
  )(x, g_pre.reshape(1, d), g_post.reshape(1, d), w_in.astype(BF16),
    w_grp.astype(BF16), scale.reshape(1, width), w_out.astype(BF16))


def _mixer_c_kernel(x_ref, gpre_ref, gpost_ref, win_ref, cw_ref, wout_ref,
                    o_ref, ext_scr, *, tm, width):
  @pl.when(pl.program_id(1) == 0)
  def _():
    ext_scr[0:SUBLANES, :] = jnp.zeros((SUBLANES, width), F32)

  x = x_ref[...]
  h = _rms(x, gpre_ref[...]).astype(BF16)
  proj = _dot(h, win_ref[...])
  bg = proj[:, :width]
  m = proj[:, width:2 * width] * proj[:, 2 * width:]
  ext_scr[SUBLANES:, :] = m
  cw = cw_ref[...]
  conv = (cw[2:3] * m
          + cw[1:2] * ext_scr[SUBLANES - 1:SUBLANES - 1 + tm, :]
          + cw[0:1] * ext_scr[SUBLANES - 2:SUBLANES - 2 + tm, :])
  ext_scr[0:SUBLANES, :] = ext_scr[tm:tm + SUBLANES, :]
  y = _dot((bg * conv).astype(BF16), wout_ref[...])
  o_ref[...] = x + _rms(y, gpost_ref[...])


def _mixer_c_layer(x, g_pre, g_post, w_in, conv_w, w_out):
  bsz, seq, d = x.shape
  width = w_out.shape[0]
  tm = TM_MIX
  kern = functools.partial(_mixer_c_kernel, tm=tm, width=width)
  full = lambda shape: pl.BlockSpec(shape, lambda b, s: (0,) * len(shape))
  return pl.pallas_call(
      kern,
      grid=(bsz, seq // tm),
      in_specs=[
          pl.BlockSpec((None, tm, d), lambda b, s: (b, s, 0)),
          full((1, d)), full((1, d)),
          full((d, 3 * width)), full((3, width)), full((width, d)),
      ],
      out_specs=pl.BlockSpec((None, tm, d), lambda b, s: (b, s, 0)),
      out_shape=jax.ShapeDtypeStruct(x.shape, x.dtype),
      scratch_shapes=[pltpu.VMEM((tm + SUBLANES, width), F32)],
      compiler_params=pltpu.CompilerParams(
          dimension_semantics=("arbitrary", "arbitrary"),
          vmem_limit_bytes=VMEM_LIMIT_BYTES),
      name="mixer_c",
  )(x, g_pre.reshape(1, d), g_post.reshape(1, d), w_in.astype(BF16), conv_w,
    w_out.astype(BF16))


_D_HALO = 32


def _mixer_d_kernel(x_ref, gpre_ref, gpost_ref, w1_ref, b1_ref, cw_ref, cb_ref,
                    lng_ref, lnb_ref, w2_ref, b2_ref, o_ref, ext_scr,
                    *, tm, width):
  @pl.when(pl.program_id(1) == 0)
  def _():
    ext_scr[0:_D_HALO, :] = jnp.zeros((_D_HALO, width), F32)

  x = x_ref[...]
  h = _rms(x, gpre_ref[...]).astype(BF16)
  pre = _dot(h, w1_ref[...]) + b1_ref[...]
  ext_scr[_D_HALO:, :] = pre[:, :width] * jax.nn.sigmoid(pre[:, width:])
  cw = cw_ref[...]
  conv = None
  for k in range(D_CONV):
    off = _D_HALO - (D_CONV - 1 - k)
    term = cw[k:k + 1] * ext_scr[off:off + tm, :]
    conv = term if conv is None else conv + term
  ext_scr[0:_D_HALO, :] = ext_scr[tm:tm + _D_HALO, :]
  z = _layernorm(conv + cb_ref[...], lng_ref[...], lnb_ref[...])
  z = (z * jax.nn.sigmoid(z)).astype(BF16)
  y = _dot(z, w2_ref[...]) + b2_ref[...]
  o_ref[...] = x + _rms(y, gpost_ref[...])


def _mixer_d_layer(x, g_pre, g_post, w1, b1, conv_w, conv_b, ln_g, ln_b, w2,
                   b2):
  bsz, seq, d = x.shape
  width = w2.shape[0]
  tm = TM_MIX
  assert conv_w.shape[0] == D_CONV
  kern = functools.partial(_mixer_d_kernel, tm=tm, width=width)
  full = lambda shape: pl.BlockSpec(shape, lambda b, s: (0,) * len(shape))
  return pl.pallas_call(
      kern,
      grid=(bsz, seq // tm),
      in_specs=[
          pl.BlockSpec((None, tm, d), lambda b, s: (b, s, 0)),
          full((1, d)), full((1, d)),
          full((d, 2 * width)), full((1, 2 * width)), full((D_CONV, width)),
          full((1, width)), full((1, width)), full((1, width)),
          full((width, d)), full((1, d)),
      ],
      out_specs=pl.BlockSpec((None, tm, d), lambda b, s: (b, s, 0)),
      out_shape=jax.ShapeDtypeStruct(x.shape, x.dtype),
      scratch_shapes=[pltpu.VMEM((tm + _D_HALO, width), F32)],
      compiler_params=pltpu.CompilerParams(
          dimension_semantics=("arbitrary", "arbitrary"),
          vmem_limit_bytes=VMEM_LIMIT_BYTES),
      name="mixer_d",
  )(x, g_pre.reshape(1, d), g_post.reshape(1, d), w1.astype(BF16),
    b1.reshape(1, -1), conv_w, conv_b.reshape(1, width), ln_g.reshape(1, width),
    ln_b.reshape(1, width), w2.astype(BF16), b2.reshape(1, d))


def kernel(x, norm_mix_pre, norm_mix_post, norm_ffn_pre, norm_ffn_post, a_w_in, a_w_s, a_b_s, a_ln_g, a_ln_b, a_w_out, b_w_in, b_w_grp, b_scale, b_w_out, c_w_in, c_conv_w, c_w_out, d_w1, d_b1, d_conv_w, d_conv_b, d_ln_g, d_ln_b, d_w2, d_b2, f_w_up, f_conv_w, f_w_down):
  depth = norm_mix_pre.shape[0]
  for i in range(depth):
    m, j = i % 4, i // 4
    gp, gq = norm_mix_pre[i], norm_mix_post[i]
    if m == 0:
      x = _mixer_a_layer(x, gp, gq, a_w_in[j], a_w_s[j], a_b_s[j], a_ln_g[j],
                         a_ln_b[j], a_w_out[j])
    elif m == 1:
      x = _mixer_b_layer(x, gp, gq, b_w_in[j], b_w_grp[j], b_scale[j],
                         b_w_out[j])
    elif m == 2:
      x = _mixer_c_layer(x, gp, gq, c_w_in[j], c_conv_w[j], c_w_out[j])
    else:
      x = _mixer_d_layer(x, gp, gq, d_w1[j], d_b1[j], d_conv_w[j], d_conv_b[j],
                         d_ln_g[j], d_ln_b[j], d_w2[j], d_b2[j])
    x = _ffn_layer(x, norm_ffn_pre[i], norm_ffn_post[i], f_w_up[i],
                   f_conv_w[i], f_w_down[i])
  return x
```

```python
import functools

import jax
import jax.numpy as jnp
from jax import lax
from jax.experimental import pallas as pl
from jax.experimental.pallas import tpu as pltpu

EPS = 1e-6
CHUNK = 64
A_BLOCK = 128
A_HEADS = 8
B_WINDOWS = (2, 4, 8, 16)
B_GROUP_DIM = 256
D_CONV = 31

V7X_VMEM_BYTES = 64 * 2**20
VMEM_LIMIT_BYTES = V7X_VMEM_BYTES - 8 * 2**20
SUBLANES = 8
LANES = 128

TM = 512
STRIP = TM // SUBLANES
FFN_FC = 256

F32 = jnp.float32
BF16 = jnp.bfloat16


def _rms(x, g):
  ms = jnp.mean(x * x, axis=-1, keepdims=True)
  return x * lax.rsqrt(ms + EPS) * g


def _layernorm(x, g, b):
  mu = jnp.mean(x, axis=-1, keepdims=True)
  xc = x - mu
  var = jnp.mean(xc * xc, axis=-1, keepdims=True)
  return xc * lax.rsqrt(var + EPS) * g + b


def _dot(a, b):
  return jnp.dot(a, b, preferred_element_type=F32)


def _to_strips(x):
  b, s, d = x.shape
  return (x.reshape(b, s // TM, SUBLANES, STRIP, d).transpose(0, 1, 3, 2, 4)
          .reshape(b, s, d))


def _from_strips(x):
  b, s, d = x.shape
  return (x.reshape(b, s // TM, STRIP, SUBLANES, d).transpose(0, 1, 3, 2, 4)
          .reshape(b, s, d))


def _fill_halo(ext, prev, hr, cols=slice(None)):
  cur_rot = ext[TM - 1:TM - 1 + hr, cols]
  prev_rot = prev[SUBLANES - 1:SUBLANES - 1 + hr, cols]
  sub = lax.broadcasted_iota(jnp.int32, cur_rot.shape, 0) % SUBLANES
  ext[0:hr, cols] = jnp.where(sub == 0, prev_rot, cur_rot)
  prev[0:hr, cols] = ext[TM:TM + hr, cols]


_FFN_HR = 2 * SUBLANES


def _ffn_kernel(x_ref, gpre_ref, gpost_ref, wup_ref, cw_ref, wdn_ref, o_ref,
                h_scr, t_scr, prev_scr, acc_scr, *, fc, nc):
  hr = _FFN_HR

  @pl.when(pl.program_id(1) == 0)
  def _():
    prev_scr[...] = jnp.zeros_like(prev_scr)

  x = x_ref[...]
  h_scr[...] = _rms(x, gpre_ref[...]).astype(BF16)

  def up(j, slot):
    t_scr[slot, hr:, :] = _dot(h_scr[...], wup_ref[j])

  def proc(j, slot):
    ext = t_scr.at[slot]
    _fill_halo(ext, prev_scr.at[j], hr)
    cw = cw_ref[j]
    c = (cw[2:3] * ext[hr:hr + TM, :]
         + cw[1:2] * ext[hr - SUBLANES:hr - SUBLANES + TM, :]
         + cw[0:1] * ext[0:TM, :])
    g = c[:, :fc]
    u = c[:, fc:]
    a = (g * jax.nn.sigmoid(g) * u).astype(BF16)
    return _dot(a, wdn_ref[j])

  up(0, 0)
  acc_scr[...] = jnp.zeros_like(acc_scr)

  def pair(i, carry):
    j = 2 * i
    up(j + 1, 1)
    y0 = proc(j, 0)
    up(j + 2, 0)
    y1 = proc(j + 1, 1)
    acc_scr[...] += y0 + y1
    return carry

  assert nc % 2 == 1
  lax.fori_loop(0, (nc - 1) // 2, pair, 0)
  y = acc_scr[...] + proc(nc - 1, 0)
  o_ref[...] = x + _rms(y, gpost_ref[...])


def _ffn_layer(x, g_pre, g_post, w_up, conv_w, w_down):
  bsz, seq, d = x.shape
  d_ff = w_down.shape[0]
  fc = FFN_FC
  nc = d_ff // fc
  assert nc * fc == d_ff and seq % TM == 0
  wup = (w_up.astype(BF16).reshape(d, 2, nc, fc).transpose(2, 0, 1, 3)
         .reshape(nc, d, 2 * fc))
  cw = conv_w.reshape(3, 2, nc, fc).transpose(2, 0, 1, 3).reshape(nc, 3, 2 * fc)
  wdn = w_down.astype(BF16).reshape(nc, fc, d)
  kern = functools.partial(_ffn_kernel, fc=fc, nc=nc)
  full = lambda shape: pl.BlockSpec(shape, lambda b, s: (0,) * len(shape))
  return pl.pallas_call(
      kern,
      grid=(bsz, seq // TM),
      in_specs=[
          pl.BlockSpec((None, TM, d), lambda b, s: (b, s, 0)),
          full((1, d)), full((1, d)),
          full((nc, d, 2 * fc)), full((nc, 3, 2 * fc)), full((nc, fc, d)),
      ],
      out_specs=pl.BlockSpec((None, TM, d), lambda b, s: (b, s, 0)),
      out_shape=jax.ShapeDtypeStruct(x.shape, x.dtype),
      scratch_shapes=[
          pltpu.VMEM((TM, d), BF16),
          pltpu.VMEM((2, TM + _FFN_HR, 2 * fc), F32),
          pltpu.VMEM((nc, _FFN_HR + SUBLANES, 2 * fc), F32),
          pltpu.VMEM((TM, d), F32),
      ],
      compiler_params=pltpu.CompilerParams(
          dimension_semantics=("arbitrary", "arbitrary"),
          vmem_limit_bytes=VMEM_LIMIT_BYTES),
      name="ffn",
  )(x, g_pre.reshape(1, d), g_post.reshape(1, d), wup, cw, wdn)


def _mixer_a_kernel(x_ref, gpre_ref, gpost_ref, win_ref, ws_ref, bias_ref,
                    lng_ref, lnb_ref, wout_ref, o_ref, z_scr, *, width):
  x = x_ref[...]
  h = _rms(x, gpre_ref[...]).astype(BF16)
  uv = _dot(h, win_ref[...])
  uv = 0.5 * uv * (1.0 + lax.erf(uv * (2.0 ** -0.5)))
  u = uv[:, :width]
  v = _layernorm(uv[:, width:], lng_ref[...], lnb_ref[...]).astype(BF16)
  q = lax.broadcasted_iota(jnp.int32, (A_BLOCK, A_BLOCK), 0) // CHUNK
  p = lax.broadcasted_iota(jnp.int32, (A_BLOCK, A_BLOCK), 1) // CHUNK
  mask = p <= q
  hd = width // A_HEADS
  for g in range(A_HEADS):
    wm = jnp.where(mask, ws_ref[g], 0.0).astype(BF16)
    for n in range(TM // A_BLOCK):
      rows = slice(n * A_BLOCK, (n + 1) * A_BLOCK)
      cols = slice(g * hd, (g + 1) * hd)
      z_scr[rows, cols] = _dot(wm, v[rows, cols]) + bias_ref[:, cols]
  y = _dot((u * z_scr[...]).astype(BF16), wout_ref[...])
  o_ref[...] = x + _rms(y, gpost_ref[...])


def _mixer_a_layer(x, g_pre, g_post, w_in, w_s, b_s, ln_g, ln_b, w_out):
  bsz, seq, d = x.shape
  width = w_out.shape[0]
  hd = width // A_HEADS
  bias = jnp.repeat(b_s.T, hd, axis=1)
  kern = functools.partial(_mixer_a_kernel, width=width)
  full = lambda shape: pl.BlockSpec(shape, lambda b, s: (0,) * len(shape))
  return pl.pallas_call(
      kern,
      grid=(bsz, seq // TM),
      in_specs=[
          pl.BlockSpec((None, TM, d), lambda b, s: (b, s, 0)),
          full((1, d)), full((1, d)),
          full((d, 2 * width)), full((A_HEADS, A_BLOCK, A_BLOCK)),
          full((A_BLOCK, width)), full((1, width)), full((1, width)),
          full((width, d)),
      ],
      out_specs=pl.BlockSpec((None, TM, d), lambda b, s: (b, s, 0)),
      out_shape=jax.ShapeDtypeStruct(x.shape, x.dtype),
      scratch_shapes=[pltpu.VMEM((TM, width), F32)],
      compiler_params=pltpu.CompilerParams(
          dimension_semantics=("arbitrary", "arbitrary"),
          vmem_limit_bytes=VMEM_LIMIT_BYTES),
      name="mixer_a",
  )(x, g_pre.reshape(1, d), g_post.reshape(1, d), w_in.astype(BF16), w_s, bias,
    ln_g.reshape(1, width), ln_b.reshape(1, width), w_out.astype(BF16))


_B_HR = (max(B_WINDOWS) - 1) * SUBLANES


def _mixer_b_kernel(x_ref, gpre_ref, gpost_ref, win_ref, wgrp_ref, scale_ref,
                    wout_ref, o_ref, ext_scr, prev_scr, mix_scr):
  hr = _B_HR
  s = pl.program_id(1)

  @pl.when(s == 0)
  def _():
    prev_scr[...] = jnp.zeros_like(prev_scr)

  x = x_ref[...]
  h = _rms(x, gpre_ref[...]).astype(BF16)
  ext_scr[hr:, :] = _dot(h, win_ref[...])
  _fill_halo(ext_scr, prev_scr, hr)
  gd = B_GROUP_DIM
  row = lax.broadcasted_iota(jnp.int32, (TM, gd), 0)
  pos = s * TM + (row % SUBLANES) * STRIP + row // SUBLANES + 1
  for g, w in enumerate(B_WINDOWS):
    cols = slice(g * gd, (g + 1) * gd)
    p = ext_scr[hr:, cols]
    win = p
    for k in range(1, w):
      win = win + ext_scr[hr - k * SUBLANES:hr - k * SUBLANES + TM, cols]
    cnt = jnp.minimum(pos, w).astype(F32)
    pooled = (win / cnt - p).astype(BF16)
    mix_scr[:, cols] = _dot(pooled, wgrp_ref[g])
  y = _dot((mix_scr[...] * scale_ref[...]).astype(BF16), wout_ref[...])
  o_ref[...] = x + _rms(y, gpost_ref[...])


def _mixer_b_layer(x, g_pre, g_post, w_in, w_grp, scale, w_out):
  bsz, seq, d = x.shape
  width = w_out.shape[0]
  ng, gd, _ = w_grp.shape
  assert gd == B_GROUP_DIM and ng == len(B_WINDOWS)
  full = lambda shape: pl.BlockSpec(shape, lambda b, s: (0,) * len(shape))
  return pl.pallas_call(
      _mixer_b_kernel,
      grid=(bsz, seq // TM),
      in_specs=[
          pl.BlockSpec((None, TM, d), lambda b, s: (b, s, 0)),
          full((1, d)), full((1, d)),
          full((d, width)), full((ng, gd, gd)), full((1, width)),
          full((width, d)),
      ],
      out_specs=pl.BlockSpec((None, TM, d), lambda b, s: (b, s, 0)),
      out_shape=jax.ShapeDtypeStruct(x.shape, x.dtype),
      scratch_shapes=[pltpu.VMEM((TM + _B_HR, width), F32),
                      pltpu.VMEM((_B_HR + SUBLANES, width), F32),
                      pltpu.VMEM((TM, width), F32)],
      compiler_params=pltpu.CompilerParams(
          dimension_semantics=("arbitrary", "arbitrary"),
          vmem_limit_bytes=VMEM_LIMIT_BYTES),
      name="mixer_b",
  )(x, g_pre.reshape(1, d), g_post.reshape(1, d), w_in.astype(BF16),
    w_grp.astype(BF16), scale.reshape(1, width), w_out.astype(BF16))


_C_HR = 2 * SUBLANES


def _mixer_c_kernel(x_ref, gpre_ref, gpost_ref, win_ref, cw_ref, wout_ref,
                    o_ref, ext_scr, prev_scr, *, width):
  hr = _C_HR

  @pl.when(pl.program_id(1) == 0)
  def _():
    prev_scr[...] = jnp.zeros_like(prev_scr)

  x = x_ref[...]
  h = _rms(x, gpre_ref[...]).astype(BF16)
  proj = _dot(h, win_ref[...])
  bg = proj[:, :width]
  m = proj[:, width:2 * width] * proj[:, 2 * width:]
  ext_scr[hr:, :] = m
  _fill_halo(ext_scr, prev_scr, hr)
  cw = cw_ref[...]
  conv = (cw[2:3] * m
          + cw[1:2] * ext_scr[hr - SUBLANES:hr - SUBLANES + TM, :]
          + cw[0:1] * ext_scr[0:TM, :])
  y = _dot((bg * conv).astype(BF16), wout_ref[...])
  o_ref[...] = x + _rms(y, gpost_ref[...])


def _mixer_c_layer(x, g_pre, g_post, w_in, conv_w, w_out):
  bsz, seq, d = x.shape
  width = w_out.shape[0]
  kern = functools.partial(_mixer_c_kernel, width=width)
  full = lambda shape: pl.BlockSpec(shape, lambda b, s: (0,) * len(shape))
  return pl.pallas_call(
      kern,
      grid=(bsz, seq // TM),
      in_specs=[
          pl.BlockSpec((None, TM, d), lambda b, s: (b, s, 0)),
          full((1, d)), full((1, d)),
          full((d, 3 * width)), full((3, width)), full((width, d)),
      ],
      out_specs=pl.BlockSpec((None, TM, d), lambda b, s: (b, s, 0)),
      out_shape=jax.ShapeDtypeStruct(x.shape, x.dtype),
      scratch_shapes=[pltpu.VMEM((TM + _C_HR, width), F32),
                      pltpu.VMEM((_C_HR + SUBLANES, width), F32)],
      compiler_params=pltpu.CompilerParams(
          dimension_semantics=("arbitrary", "arbitrary"),
          vmem_limit_bytes=VMEM_LIMIT_BYTES),
      name="mixer_c",
  )(x, g_pre.reshape(1, d), g_post.reshape(1, d), w_in.astype(BF16), conv_w,
    w_out.astype(BF16))


_D_HR = (D_CONV - 1) * SUBLANES
_D_CB = 256
_D_ROWS = 16


def _mixer_d_kernel(x_ref, gpre_ref, gpost_ref, w1_ref, b1_ref, cw_ref, cb_ref,
                    lng_ref, lnb_ref, w2_ref, b2_ref, o_ref, ext_scr, prev_scr,
                    conv_scr, *, width):
  hr = _D_HR

  @pl.when(pl.program_id(1) == 0)
  def _():
    prev_scr[...] = jnp.zeros_like(prev_scr)

  x = x_ref[...]
  h = _rms(x, gpre_ref[...]).astype(BF16)

  order_bits = None
  for cb in range(width // _D_CB):
    cs = slice(cb * _D_CB, (cb + 1) * _D_CB)
    gs = slice(width + cb * _D_CB, width + (cb + 1) * _D_CB)
    a = _dot(h, w1_ref[:, cs]) + b1_ref[:, cs]
    gate = _dot(h, w1_ref[:, gs]) + b1_ref[:, gs]
    ext_scr[hr:, cs] = a * jax.nn.sigmoid(gate)
    _fill_halo(ext_scr, prev_scr, hr, cs)
    for lb in range(cb * _D_CB // LANES, (cb + 1) * _D_CB // LANES):
      cols = slice(lb * LANES, (lb + 1) * LANES)
      for i0 in range(0, STRIP, _D_ROWS):
        accs = [None] * _D_ROWS
        for k in range(D_CONV - 1, -1, -1):
          tap = cw_ref[k, :, cols]
          if order_bits is not None:
            tap = pltpu.bitcast(pltpu.bitcast(tap, jnp.uint32) | order_bits, F32)
          for g in range(_D_ROWS):
            off = (i0 + g + k) * SUBLANES
            term = tap * ext_scr[off:off + SUBLANES, cols]
            accs[g] = term if accs[g] is None else accs[g] + term
        for g in range(_D_ROWS):
          base = (i0 + g) * SUBLANES
          conv_scr[base:base + SUBLANES, cols] = accs[g]
        order_bits = (pltpu.bitcast(accs[0], jnp.uint32) >> 16) >> 16

  z = _layernorm(conv_scr[...] + cb_ref[...], lng_ref[...], lnb_ref[...])
  z = (z * jax.nn.sigmoid(z)).astype(BF16)
  y = _dot(z, w2_ref[...]) + b2_ref[...]
  o_ref[...] = x + _rms(y, gpost_ref[...])


def _mixer_d_layer(x, g_pre, g_post, w1, b1, conv_w, conv_b, ln_g, ln_b, w2,
                   b2):
  bsz, seq, d = x.shape
  width = w2.shape[0]
  assert conv_w.shape[0] == D_CONV and D_CONV - 1 <= STRIP
  kern = functools.partial(_mixer_d_kernel, width=width)
  full = lambda shape: pl.BlockSpec(shape, lambda b, s: (0,) * len(shape))
  return pl.pallas_call(
      kern,
      grid=(bsz, seq // TM),
      in_specs=[
          pl.BlockSpec((None, TM, d), lambda b, s: (b, s, 0)),
          full((1, d)), full((1, d)),
          full((d, 2 * width)), full((1, 2 * width)),
          full((D_CONV, SUBLANES, width)),
          full((1, width)), full((1, width)), full((1, width)),
          full((width, d)), full((1, d)),
      ],
      out_specs=pl.BlockSpec((None, TM, d), lambda b, s: (b, s, 0)),
      out_shape=jax.ShapeDtypeStruct(x.shape, x.dtype),
      scratch_shapes=[pltpu.VMEM((TM + _D_HR, width), F32),
                      pltpu.VMEM((_D_HR + SUBLANES, width), F32),
                      pltpu.VMEM((TM, width), F32)],
      compiler_params=pltpu.CompilerParams(
          dimension_semantics=("arbitrary", "arbitrary"),
          vmem_limit_bytes=VMEM_LIMIT_BYTES),
      name="mixer_d",
  )(x, g_pre.reshape(1, d), g_post.reshape(1, d), w1.astype(BF16),
    b1.reshape(1, -1),
    jnp.broadcast_to(conv_w[:, None, :], (D_CONV, SUBLANES, width)),
    conv_b.reshape(1, width), ln_g.reshape(1, width),
    ln_b.reshape(1, width), w2.astype(BF16), b2.reshape(1, d))


def kernel(x, norm_mix_pre, norm_mix_post, norm_ffn_pre, norm_ffn_post, a_w_in, a_w_s, a_b_s, a_ln_g, a_ln_b, a_w_out, b_w_in, b_w_grp, b_scale, b_w_out, c_w_in, c_conv_w, c_w_out, d_w1, d_b1, d_conv_w, d_conv_b, d_ln_g, d_ln_b, d_w2, d_b2, f_w_up, f_conv_w, f_w_down):
  depth = norm_mix_pre.shape[0]
  strips = False

  def want(flag, x, strips):
    if flag and not strips:
      return _to_strips(x), True
    if strips and not flag:
      return _from_strips(x), False
    return x, strips

  for i in range(depth):
    m, j = i % 4, i // 4
    gp, gq = norm_mix_pre[i], norm_mix_post[i]
    x, strips = want(m != 0, x, strips)
    if m == 0:
      x = _mixer_a_layer(x, gp, gq, a_w_in[j], a_w_s[j], a_b_s[j], a_ln_g[j],
                         a_ln_b[j], a_w_out[j])
    elif m == 1:
      x = _mixer_b_layer(x, gp, gq, b_w_in[j], b_w_grp[j], b_scale[j],
                         b_w_out[j])
    elif m == 2:
      x = _mixer_c_layer(x, gp, gq, c_w_in[j], c_conv_w[j], c_w_out[j])
    else:
      x = _mixer_d_layer(x, gp, gq, d_w1[j], d_b1[j], d_conv_w[j], d_conv_b[j],
                         d_ln_g[j], d_ln_b[j], d_w2[j], d_b2[j])
    x, strips = want(True, x, strips)
    x = _ffn_layer(x, norm_ffn_pre[i], norm_ffn_post[i], f_w_up[i],
                   f_conv_w[i], f_w_down[i])
  x, strips = want(False, x, strips)
  return x
```

```python
import functools

import jax
import jax.numpy as jnp
from jax import lax
from jax.experimental import pallas as pl
from jax.experimental.pallas import tpu as pltpu

EPS = 1e-6
CHUNK = 64
A_BLOCK = 128
A_HEADS = 8
B_WINDOWS = (2, 4, 8, 16)
B_GROUP_DIM = 256
D_CONV = 31

V7X_VMEM_BYTES = 64 * 2**20
VMEM_LIMIT_BYTES = V7X_VMEM_BYTES - 8 * 2**20
SUBLANES = 8
LANES = 128

TM = 512
STRIP = TM // SUBLANES
FFN_FC = 256

F32 = jnp.float32
BF16 = jnp.bfloat16


def _rms(x, g):
  ms = jnp.mean(x * x, axis=-1, keepdims=True)
  return x * lax.rsqrt(ms + EPS) * g


def _layernorm(x, g, b):
  mu = jnp.mean(x, axis=-1, keepdims=True)
  xc = x - mu
  var = jnp.mean(xc * xc, axis=-1, keepdims=True)
  return xc * lax.rsqrt(var + EPS) * g + b


def _dot(a, b):
  return jnp.dot(a, b, preferred_element_type=F32)


def _to_strips(x):
  b, s, d = x.shape
  return (x.reshape(b, s // TM, SUBLANES, STRIP, d).transpose(0, 1, 3, 2, 4)
          .reshape(b, s, d))


def _from_strips(x):
  b, s, d = x.shape
  return (x.reshape(b, s // TM, STRIP, SUBLANES, d).transpose(0, 1, 3, 2, 4)
          .reshape(b, s, d))


def _fill_halo(ext, prev, hr, cols=slice(None)):
  cur_rot = ext[TM - 1:TM - 1 + hr, cols]
  prev_rot = prev[SUBLANES - 1:SUBLANES - 1 + hr, cols]
  sub = lax.broadcasted_iota(jnp.int32, cur_rot.shape, 0) % SUBLANES
  ext[0:hr, cols] = jnp.where(sub == 0, prev_rot, cur_rot)
  prev[0:hr, cols] = ext[TM:TM + hr, cols]


_FFN_HR = 2 * SUBLANES


def _ffn_kernel(x_ref, gpre_ref, gpost_ref, wup_ref, cw_ref, wdn_ref, o_ref,
                h_scr, t_scr, prev_scr, *, fc, nc):
  hr = _FFN_HR
  d_ff = fc * nc

  @pl.when(pl.program_id(1) == 0)
  def _():
    prev_scr[...] = jnp.zeros_like(prev_scr)

  x = x_ref[...]
  h_scr[...] = _rms(x, gpre_ref[...]).astype(BF16)

  def cols(j):
    return (slice(j * fc, (j + 1) * fc),
            slice(d_ff + j * fc, d_ff + (j + 1) * fc))

  def up(j, slot):
    for half, cs in enumerate(cols(j)):
      t_scr[slot, hr:, half * fc:(half + 1) * fc] = _dot(h_scr[...],
                                                         wup_ref[:, cs])

  def proc(j, slot):
    ext = t_scr.at[slot]
    _fill_halo(ext, prev_scr.at[j], hr)
    halves = []
    for half, cs in enumerate(cols(j)):
      lanes = slice(half * fc, (half + 1) * fc)
      halves.append(cw_ref[2:3, cs] * ext[hr:hr + TM, lanes]
                    + cw_ref[1:2, cs] * ext[hr - SUBLANES:hr - SUBLANES + TM, lanes]
                    + cw_ref[0:1, cs] * ext[0:TM, lanes])
    g, u = halves
    a = (g * jax.nn.sigmoid(g) * u).astype(BF16)
    return _dot(a, wdn_ref[j * fc:(j + 1) * fc, :])

  up(0, 0)
  y = None
  for j in range(nc):
    if j + 1 < nc:
      up(j + 1, (j + 1) % 2)
    yj = proc(j, j % 2)
    y = yj if y is None else y + yj
  o_ref[...] = x + _rms(y, gpost_ref[...])


def _ffn_layer(x, g_pre, g_post, w_up, conv_w, w_down):
  bsz, seq, d = x.shape
  d_ff = w_down.shape[0]
  fc = FFN_FC
  nc = d_ff // fc
  assert nc * fc == d_ff and seq % TM == 0
  kern = functools.partial(_ffn_kernel, fc=fc, nc=nc)
  full = lambda shape: pl.BlockSpec(shape, lambda b, s: (0,) * len(shape))
  return pl.pallas_call(
      kern,
      grid=(bsz, seq // TM),
      in_specs=[
          pl.BlockSpec((None, TM, d), lambda b, s: (b, s, 0)),
          full((1, d)), full((1, d)),
          full((d, 2 * d_ff)), full((3, 2 * d_ff)), full((d_ff, d)),
      ],
      out_specs=pl.BlockSpec((None, TM, d), lambda b, s: (b, s, 0)),
      out_shape=jax.ShapeDtypeStruct(x.shape, x.dtype),
      scratch_shapes=[
          pltpu.VMEM((TM, d), BF16),
          pltpu.VMEM((2, TM + _FFN_HR, 2 * fc), F32),
          pltpu.VMEM((nc, _FFN_HR + SUBLANES, 2 * fc), F32),
      ],
      compiler_params=pltpu.CompilerParams(
          dimension_semantics=("arbitrary", "arbitrary"),
          vmem_limit_bytes=VMEM_LIMIT_BYTES),
      name="ffn",
  )(x, g_pre.reshape(1, d), g_post.reshape(1, d), w_up.astype(BF16), conv_w,
    w_down.astype(BF16))


def _mixer_a_kernel(x_ref, gpre_ref, gpost_ref, win_ref, ws_ref, bias_ref,
                    lng_ref, lnb_ref, wout_ref, o_ref, z_scr, *, width):
  x = x_ref[...]
  h = _rms(x, gpre_ref[...]).astype(BF16)
  uv = _dot(h, win_ref[...])
  uv = 0.5 * uv * (1.0 + lax.erf(uv * (2.0 ** -0.5)))
  u = uv[:, :width]
  v = _layernorm(uv[:, width:], lng_ref[...], lnb_ref[...]).astype(BF16)
  q = lax.broadcasted_iota(jnp.int32, (A_BLOCK, A_BLOCK), 0) // CHUNK
  p = lax.broadcasted_iota(jnp.int32, (A_BLOCK, A_BLOCK), 1) // CHUNK
  mask = p <= q
  hd = width // A_HEADS
  for g in range(A_HEADS):
    wm = jnp.where(mask, ws_ref[g], 0.0).astype(BF16)
    for n in range(TM // A_BLOCK):
      rows = slice(n * A_BLOCK, (n + 1) * A_BLOCK)
      cols = slice(g * hd, (g + 1) * hd)
      z_scr[rows, cols] = _dot(wm, v[rows, cols]) + bias_ref[:, cols]
  y = _dot((u * z_scr[...]).astype(BF16), wout_ref[...])
  o_ref[...] = x + _rms(y, gpost_ref[...])


def _mixer_a_layer(x, g_pre, g_post, w_in, w_s, b_s, ln_g, ln_b, w_out):
  bsz, seq, d = x.shape
  width = w_out.shape[0]
  hd = width // A_HEADS
  bias = jnp.repeat(b_s.T, hd, axis=1)
  kern = functools.partial(_mixer_a_kernel, width=width)
  full = lambda shape: pl.BlockSpec(shape, lambda b, s: (0,) * len(shape))
  return pl.pallas_call(
      kern,
      grid=(bsz, seq // TM),
      in_specs=[
          pl.BlockSpec((None, TM, d), lambda b, s: (b, s, 0)),
          full((1, d)), full((1, d)),
          full((d, 2 * width)), full((A_HEADS, A_BLOCK, A_BLOCK)),
          full((A_BLOCK, width)), full((1, width)), full((1, width)),
          full((width, d)),
      ],
      out_specs=pl.BlockSpec((None, TM, d), lambda b, s: (b, s, 0)),
      out_shape=jax.ShapeDtypeStruct(x.shape, x.dtype),
      scratch_shapes=[pltpu.VMEM((TM, width), F32)],
      compiler_params=pltpu.CompilerParams(
          dimension_semantics=("arbitrary", "arbitrary"),
          vmem_limit_bytes=VMEM_LIMIT_BYTES),
      name="mixer_a",
  )(x, g_pre.reshape(1, d), g_post.reshape(1, d), w_in.astype(BF16), w_s, bias,
    ln_g.reshape(1, width), ln_b.reshape(1, width), w_out.astype(BF16))


_B_HR = (max(B_WINDOWS) - 1) * SUBLANES


def _mixer_b_kernel(x_ref, gpre_ref, gpost_ref, win_ref, wgrp_ref, scale_ref,
                    wout_ref, o_ref, ext_scr, prev_scr, mix_scr):
  hr = _B_HR
  s = pl.program_id(1)

  @pl.when(s == 0)
  def _():
    prev_scr[...] = jnp.zeros_like(prev_scr)

  x = x_ref[...]
  h = _rms(x, gpre_ref[...]).astype(BF16)
  ext_scr[hr:, :] = _dot(h, win_ref[...])
  _fill_halo(ext_scr, prev_scr, hr)
  gd = B_GROUP_DIM
  row = lax.broadcasted_iota(jnp.int32, (TM, gd), 0)
  pos = s * TM + (row % SUBLANES) * STRIP + row // SUBLANES + 1
  for g, w in enumerate(B_WINDOWS):
    cols = slice(g * gd, (g + 1) * gd)
    p = ext_scr[hr:, cols]
    win = p
    for k in range(1, w):
      win = win + ext_scr[hr - k * SUBLANES:hr - k * SUBLANES + TM, cols]
    cnt = jnp.minimum(pos, w).astype(F32)
    pooled = (win / cnt - p).astype(BF16)
    mix_scr[:, cols] = _dot(pooled, wgrp_ref[g])
  y = _dot((mix_scr[...] * scale_ref[...]).astype(BF16), wout_ref[...])
  o_ref[...] = x + _rms(y, gpost_ref[...])


def _mixer_b_layer(x, g_pre, g_post, w_in, w_grp, scale, w_out):
  bsz, seq, d = x.shape
  width = w_out.shape[0]
  ng, gd, _ = w_grp.shape
  assert gd == B_GROUP_DIM and ng == len(B_WINDOWS)
  full = lambda shape: pl.BlockSpec(shape, lambda b, s: (0,) * len(shape))
  return pl.pallas_call(
      _mixer_b_kernel,
      grid=(bsz, seq // TM),
      in_specs=[
          pl.BlockSpec((None, TM, d), lambda b, s: (b, s, 0)),
          full((1, d)), full((1, d)),
          full((d, width)), full((ng, gd, gd)), full((1, width)),
          full((width, d)),
      ],
      out_specs=pl.BlockSpec((None, TM, d), lambda b, s: (b, s, 0)),
      out_shape=jax.ShapeDtypeStruct(x.shape, x.dtype),
      scratch_shapes=[pltpu.VMEM((TM + _B_HR, width), F32),
                      pltpu.VMEM((_B_HR + SUBLANES, width), F32),
                      pltpu.VMEM((TM, width), F32)],
      compiler_params=pltpu.CompilerParams(
          dimension_semantics=("arbitrary", "arbitrary"),
          vmem_limit_bytes=VMEM_LIMIT_BYTES),
      name="mixer_b",
  )(x, g_pre.reshape(1, d), g_post.reshape(1, d), w_in.astype(BF16),
    w_grp.astype(BF16), scale.reshape(1, width), w_out.astype(BF16))


_C_HR = 2 * SUBLANES


def _mixer_c_kernel(x_ref, gpre_ref, gpost_ref, win_ref, cw_ref, wout_ref,
                    o_ref, ext_scr, prev_scr, *, width):
  hr = _C_HR

  @pl.when(pl.program_id(1) == 0)
  def _():
    prev_scr[...] = jnp.zeros_like(prev_scr)

  x = x_ref[...]
  h = _rms(x, gpre_ref[...]).astype(BF16)
  proj = _dot(h, win_ref[...])
  bg = proj[:, :width]
  m = proj[:, width:2 * width] * proj[:, 2 * width:]
  ext_scr[hr:, :] = m
  _fill_halo(ext_scr, prev_scr, hr)
  cw = cw_ref[...]
  conv = (cw[2:3] * m
          + cw[1:2] * ext_scr[hr - SUBLANES:hr - SUBLANES + TM, :]
          + cw[0:1] * ext_scr[0:TM, :])
  y = _dot((bg * conv).astype(BF16), wout_ref[...])
  o_ref[...] = x + _rms(y, gpost_ref[...])


def _mixer_c_layer(x, g_pre, g_post, w_in, conv_w, w_out):
  bsz, seq, d = x.shape
  width = w_out.shape[0]
  kern = functools.partial(_mixer_c_kernel, width=width)
  full = lambda shape: pl.BlockSpec(shape, lambda b, s: (0,) * len(shape))
  return pl.pallas_call(
      kern,
      grid=(bsz, seq // TM),
      in_specs=[
          pl.BlockSpec((None, TM, d), lambda b, s: (b, s, 0)),
          full((1, d)), full((1, d)),
          full((d, 3 * width)), full((3, width)), full((width, d)),
      ],
      out_specs=pl.BlockSpec((None, TM, d), lambda b, s: (b, s, 0)),
      out_shape=jax.ShapeDtypeStruct(x.shape, x.dtype),
      scratch_shapes=[pltpu.VMEM((TM + _C_HR, width), F32),
                      pltpu.VMEM((_C_HR + SUBLANES, width), F32)],
      compiler_params=pltpu.CompilerParams(
          dimension_semantics=("arbitrary", "arbitrary"),
          vmem_limit_bytes=VMEM_LIMIT_BYTES),
      name="mixer_c",
  )(x, g_pre.reshape(1, d), g_post.reshape(1, d), w_in.astype(BF16), conv_w,
    w_out.astype(BF16))


_D_HR = (D_CONV - 1) * SUBLANES
_D_CB = 256
_D_ROWS = 16


def _mixer_d_kernel(x_ref, gpre_ref, gpost_ref, w1_ref, b1_ref, cw_ref, cb_ref,
                    lng_ref, lnb_ref, w2_ref, b2_ref, o_ref, ext_scr, prev_scr,
                    conv_scr, *, width):
  hr = _D_HR

  @pl.when(pl.program_id(1) == 0)
  def _():
    prev_scr[...] = jnp.zeros_like(prev_scr)

  x = x_ref[...]
  h = _rms(x, gpre_ref[...]).astype(BF16)

  order_bits = None
  for cb in range(width // _D_CB):
    cs = slice(cb * _D_CB, (cb + 1) * _D_CB)
    gs = slice(width + cb * _D_CB, width + (cb + 1) * _D_CB)
    a = _dot(h, w1_ref[:, cs]) + b1_ref[:, cs]
    gate = _dot(h, w1_ref[:, gs]) + b1_ref[:, gs]
    ext_scr[hr:, cs] = a * jax.nn.sigmoid(gate)
    _fill_halo(ext_scr, prev_scr, hr, cs)
    for lb in range(cb * _D_CB // LANES, (cb + 1) * _D_CB // LANES):
      cols = slice(lb * LANES, (lb + 1) * LANES)
      for i0 in range(0, STRIP, _D_ROWS):
        accs = [None] * _D_ROWS
        for k in range(D_CONV - 1, -1, -1):
          tap = cw_ref[k, :, cols]
          if order_bits is not None:
            tap = pltpu.bitcast(pltpu.bitcast(tap, jnp.uint32) | order_bits, F32)
          for g in range(_D_ROWS):
            off = (i0 + g + k) * SUBLANES
            term = tap * ext_scr[off:off + SUBLANES, cols]
            accs[g] = term if accs[g] is None else accs[g] + term
        for g in range(_D_ROWS):
          base = (i0 + g) * SUBLANES
          conv_scr[base:base + SUBLANES, cols] = accs[g]
        order_bits = (pltpu.bitcast(accs[0], jnp.uint32) >> 16) >> 16

  z = _layernorm(conv_scr[...] + cb_ref[...], lng_ref[...], lnb_ref[...])
  z = (z * jax.nn.sigmoid(z)).astype(BF16)
  y = _dot(z, w2_ref[...]) + b2_ref[...]
  o_ref[...] = x + _rms(y, gpost_ref[...])


def _mixer_d_layer(x, g_pre, g_post, w1, b1, conv_w, conv_b, ln_g, ln_b, w2,
                   b2):
  bsz, seq, d = x.shape
  width = w2.shape[0]
  assert conv_w.shape[0] == D_CONV and D_CONV - 1 <= STRIP
  kern = functools.partial(_mixer_d_kernel, width=width)
  full = lambda shape: pl.BlockSpec(shape, lambda b, s: (0,) * len(shape))
  return pl.pallas_call(
      kern,
      grid=(bsz, seq // TM),
      in_specs=[
          pl.BlockSpec((None, TM, d), lambda b, s: (b, s, 0)),
          full((1, d)), full((1, d)),
          full((d, 2 * width)), full((1, 2 * width)),
          full((D_CONV, SUBLANES, width)),
          full((1, width)), full((1, width)), full((1, width)),
          full((width, d)), full((1, d)),
      ],
      out_specs=pl.BlockSpec((None, TM, d), lambda b, s: (b, s, 0)),
      out_shape=jax.ShapeDtypeStruct(x.shape, x.dtype),
      scratch_shapes=[pltpu.VMEM((TM + _D_HR, width), F32),
                      pltpu.VMEM((_D_HR + SUBLANES, width), F32),
                      pltpu.VMEM((TM, width), F32)],
      compiler_params=pltpu.CompilerParams(
          dimension_semantics=("arbitrary", "arbitrary"),
          vmem_limit_bytes=VMEM_LIMIT_BYTES),
      name="mixer_d",
  )(x, g_pre.reshape(1, d), g_post.reshape(1, d), w1.astype(BF16),
    b1.reshape(1, -1),
    jnp.broadcast_to(conv_w[:, None, :], (D_CONV, SUBLANES, width)),
    conv_b.reshape(1, width), ln_g.reshape(1, width),
    ln_b.reshape(1, width), w2.astype(BF16), b2.reshape(1, d))


def kernel(x, norm_mix_pre, norm_mix_post, norm_ffn_pre, norm_ffn_post, a_w_in, a_w_s, a_b_s, a_ln_g, a_ln_b, a_w_out, b_w_in, b_w_grp, b_scale, b_w_out, c_w_in, c_conv_w, c_w_out, d_w1, d_b1, d_conv_w, d_conv_b, d_ln_g, d_ln_b, d_w2, d_b2, f_w_up, f_conv_w, f_w_down):
  depth = norm_mix_pre.shape[0]
  strips = False

  def want(flag, x, strips):
    if flag and not strips:
      return _to_strips(x), True
    if strips and not flag:
      return _from_strips(x), False
    return x, strips

  for i in range(depth):
    m, j = i % 4, i // 4
    gp, gq = norm_mix_pre[i], norm_mix_post[i]
    x, strips = want(m != 0, x, strips)
    if m == 0:
      x = _mixer_a_layer(x, gp, gq, a_w_in[j], a_w_s[j], a_b_s[j], a_ln_g[j],
                         a_ln_b[j], a_w_out[j])
    elif m == 1:
      x = _mixer_b_layer(x, gp, gq, b_w_in[j], b_w_grp[j], b_scale[j],
                         b_w_out[j])
    elif m == 2:
      x = _mixer_c_layer(x, gp, gq, c_w_in[j], c_conv_w[j], c_w_out[j])
    else:
      x = _mixer_d_layer(x, gp, gq, d_w1[j], d_b1[j], d_conv_w[j], d_conv_b[j],
                         d_ln_g[j], d_ln_b[j], d_w2[j], d_b2[j])
    x, strips = want(True, x, strips)
    x = _ffn_layer(x, norm_ffn_pre[i], norm_ffn_post[i], f_w_up[i],
                   f_conv_w[i], f_w_down[i])
  x, strips = want(False, x, strips)
  return x
```

```python
import functools

import jax
import jax.numpy as jnp
from jax import lax
from jax.experimental import pallas as pl
from jax.experimental.pallas import tpu as pltpu

EPS = 1e-6
CHUNK = 64
A_BLOCK = 128
A_HEADS = 8
B_WINDOWS = (2, 4, 8, 16)
B_GROUP_DIM = 256
D_CONV = 31

V7X_VMEM_BYTES = 64 * 2**20
VMEM_LIMIT_BYTES = V7X_VMEM_BYTES - 8 * 2**20
SUBLANES = 8
LANES = 128

TM = 512
STRIP = TM // SUBLANES
FFN_FC = 256

F32 = jnp.float32
BF16 = jnp.bfloat16


def _rms(x, g):
  ms = jnp.mean(x * x, axis=-1, keepdims=True)
  return x * lax.rsqrt(ms + EPS) * g


def _layernorm(x, g, b):
  mu = jnp.mean(x, axis=-1, keepdims=True)
  xc = x - mu
  var = jnp.mean(xc * xc, axis=-1, keepdims=True)
  return xc * lax.rsqrt(var + EPS) * g + b


def _dot(a, b):
  return jnp.dot(a, b, preferred_element_type=F32)


def _to_strips(x):
  b, s, d = x.shape
  return (x.reshape(b, s // TM, SUBLANES, STRIP, d).transpose(0, 1, 3, 2, 4)
          .reshape(b, s, d))


def _from_strips(x):
  b, s, d = x.shape
  return (x.reshape(b, s // TM, STRIP, SUBLANES, d).transpose(0, 1, 3, 2, 4)
          .reshape(b, s, d))


def _fill_halo(ext, prev, hr, cols=slice(None)):
  cur_rot = ext[TM - 1:TM - 1 + hr, cols]
  prev_rot = prev[SUBLANES - 1:SUBLANES - 1 + hr, cols]
  sub = lax.broadcasted_iota(jnp.int32, cur_rot.shape, 0) % SUBLANES
  ext[0:hr, cols] = jnp.where(sub == 0, prev_rot, cur_rot)
  prev[0:hr, cols] = ext[TM:TM + hr, cols]


_FFN_HR = 2 * SUBLANES


def _ffn_kernel(x_ref, gpre_ref, gpost_ref, wup_ref, cw_ref, wdn_ref, o_ref,
                h_scr, t_scr, prev_scr, *, fc, nc):
  hr = _FFN_HR
  d_ff = fc * nc

  @pl.when(pl.program_id(1) == 0)
  def _():
    prev_scr[...] = jnp.zeros_like(prev_scr)

  x = x_ref[...]
  h_scr[...] = _rms(x, gpre_ref[...]).astype(BF16)

  def cols(j):
    return (slice(j * fc, (j + 1) * fc),
            slice(d_ff + j * fc, d_ff + (j + 1) * fc))

  def up(j, slot):
    for half, cs in enumerate(cols(j)):
      t_scr[slot, hr:, half * fc:(half + 1) * fc] = _dot(h_scr[...],
                                                         wup_ref[:, cs])

  def proc(j, slot):
    ext = t_scr.at[slot]
    _fill_halo(ext, prev_scr.at[j], hr)
    halves = []
    for half, cs in enumerate(cols(j)):
      lanes = slice(half * fc, (half + 1) * fc)
      halves.append(cw_ref[2:3, cs] * ext[hr:hr + TM, lanes]
                    + cw_ref[1:2, cs] * ext[hr - SUBLANES:hr - SUBLANES + TM, lanes]
                    + cw_ref[0:1, cs] * ext[0:TM, lanes])
    g, u = halves
    a = (g * jax.nn.sigmoid(g) * u).astype(BF16)
    return _dot(a, wdn_ref[j * fc:(j + 1) * fc, :])

  up(0, 0)
  y = None
  for j in range(nc):
    if j + 1 < nc:
      up(j + 1, (j + 1) % 2)
    yj = proc(j, j % 2)
    y = yj if y is None else y + yj
  o_ref[...] = x + _rms(y, gpost_ref[...])


def _ffn_layer(x, g_pre, g_post, w_up, conv_w, w_down):
  bsz, seq, d = x.shape
  d_ff = w_down.shape[0]
  fc = FFN_FC
  nc = d_ff // fc
  assert nc * fc == d_ff and seq % TM == 0
  kern = functools.partial(_ffn_kernel, fc=fc, nc=nc)
  full = lambda shape: pl.BlockSpec(shape, lambda b, s: (0,) * len(shape))
  return pl.pallas_call(
      kern,
      grid=(bsz, seq // TM),
      in_specs=[
          pl.BlockSpec((None, TM, d), lambda b, s: (b, s, 0)),
          full((1, d)), full((1, d)),
          full((d, 2 * d_ff)), full((3, 2 * d_ff)), full((d_ff, d)),
      ],
      out_specs=pl.BlockSpec((None, TM, d), lambda b, s: (b, s, 0)),
      out_shape=jax.ShapeDtypeStruct(x.shape, x.dtype),
      scratch_shapes=[
          pltpu.VMEM((TM, d), BF16),
          pltpu.VMEM((2, TM + _FFN_HR, 2 * fc), F32),
          pltpu.VMEM((nc, _FFN_HR + SUBLANES, 2 * fc), F32),
      ],
      compiler_params=pltpu.CompilerParams(
          dimension_semantics=("arbitrary", "arbitrary"),
          vmem_limit_bytes=VMEM_LIMIT_BYTES),
      name="ffn",
  )(x, g_pre.reshape(1, d), g_post.reshape(1, d), w_up.astype(BF16), conv_w,
    w_down.astype(BF16))


def _mixer_a_kernel(x_ref, gpre_ref, gpost_ref, win_ref, ws_ref, bias_ref,
                    lng_ref, lnb_ref, wout_ref, o_ref, z_scr, *, width):
  x = x_ref[...]
  h = _rms(x, gpre_ref[...]).astype(BF16)
  uv = _dot(h, win_ref[...])
  uv = 0.5 * uv * (1.0 + lax.erf(uv * (2.0 ** -0.5)))
  u = uv[:, :width]
  v = _layernorm(uv[:, width:], lng_ref[...], lnb_ref[...]).astype(BF16)
  q = lax.broadcasted_iota(jnp.int32, (A_BLOCK, A_BLOCK), 0) // CHUNK
  p = lax.broadcasted_iota(jnp.int32, (A_BLOCK, A_BLOCK), 1) // CHUNK
  mask = p <= q
  hd = width // A_HEADS
  for g in range(A_HEADS):
    wm = jnp.where(mask, ws_ref[g], 0.0).astype(BF16)
    for n in range(TM // A_BLOCK):
      rows = slice(n * A_BLOCK, (n + 1) * A_BLOCK)
      cols = slice(g * hd, (g + 1) * hd)
      z_scr[rows, cols] = _dot(wm, v[rows, cols]) + bias_ref[:, cols]
  y = _dot((u * z_scr[...]).astype(BF16), wout_ref[...])
  o_ref[...] = x + _rms(y, gpost_ref[...])


def _mixer_a_layer(x, g_pre, g_post, w_in, w_s, b_s, ln_g, ln_b, w_out):
  bsz, seq, d = x.shape
  width = w_out.shape[0]
  hd = width // A_HEADS
  bias = jnp.repeat(b_s.T, hd, axis=1)
  kern = functools.partial(_mixer_a_kernel, width=width)
  full = lambda shape: pl.BlockSpec(shape, lambda b, s: (0,) * len(shape))
  return pl.pallas_call(
      kern,
      grid=(bsz, seq // TM),
      in_specs=[
          pl.BlockSpec((None, TM, d), lambda b, s: (b, s, 0)),
          full((1, d)), full((1, d)),
          full((d, 2 * width)), full((A_HEADS, A_BLOCK, A_BLOCK)),
          full((A_BLOCK, width)), full((1, width)), full((1, width)),
          full((width, d)),
      ],
      out_specs=pl.BlockSpec((None, TM, d), lambda b, s: (b, s, 0)),
      out_shape=jax.ShapeDtypeStruct(x.shape, x.dtype),
      scratch_shapes=[pltpu.VMEM((TM, width), F32)],
      compiler_params=pltpu.CompilerParams(
          dimension_semantics=("arbitrary", "arbitrary"),
          vmem_limit_bytes=VMEM_LIMIT_BYTES),
      name="mixer_a",
  )(x, g_pre.reshape(1, d), g_post.reshape(1, d), w_in.astype(BF16), w_s, bias,
    ln_g.reshape(1, width), ln_b.reshape(1, width), w_out.astype(BF16))


_B_HR = (max(B_WINDOWS) - 1) * SUBLANES


def _mixer_b_kernel(x_ref, gpre_ref, gpost_ref, win_ref, wgrp_ref, scale_ref,
                    wout_ref, o_ref, ext_scr, prev_scr, mix_scr):
  hr = _B_HR
  s = pl.program_id(1)

  @pl.when(s == 0)
  def _():
    prev_scr[...] = jnp.zeros_like(prev_scr)

  x = x_ref[...]
  h = _rms(x, gpre_ref[...]).astype(BF16)
  ext_scr[hr:, :] = _dot(h, win_ref[...])
  _fill_halo(ext_scr, prev_scr, hr)
  gd = B_GROUP_DIM
  row = lax.broadcasted_iota(jnp.int32, (TM, gd), 0)
  pos = s * TM + (row % SUBLANES) * STRIP + row // SUBLANES + 1
  for g, w in enumerate(B_WINDOWS):
    cols = slice(g * gd, (g + 1) * gd)
    p = ext_scr[hr:, cols]
    win = p
    for k in range(1, w):
      win = win + ext_scr[hr - k * SUBLANES:hr - k * SUBLANES + TM, cols]
    cnt = jnp.minimum(pos, w).astype(F32)
    pooled = (win / cnt - p).astype(BF16)
    mix_scr[:, cols] = _dot(pooled, wgrp_ref[g])
  y = _dot((mix_scr[...] * scale_ref[...]).astype(BF16), wout_ref[...])
  o_ref[...] = x + _rms(y, gpost_ref[...])


def _mixer_b_layer(x, g_pre, g_post, w_in, w_grp, scale, w_out):
  bsz, seq, d = x.shape
  width = w_out.shape[0]
  ng, gd, _ = w_grp.shape
  assert gd == B_GROUP_DIM and ng == len(B_WINDOWS)
  full = lambda shape: pl.BlockSpec(shape, lambda b, s: (0,) * len(shape))
  return pl.pallas_call(
      _mixer_b_kernel,
      grid=(bsz, seq // TM),
      in_specs=[
          pl.BlockSpec((None, TM, d), lambda b, s: (b, s, 0)),
          full((1, d)), full((1, d)),
          full((d, width)), full((ng, gd, gd)), full((1, width)),
          full((width, d)),
      ],
      out_specs=pl.BlockSpec((None, TM, d), lambda b, s: (b, s, 0)),
      out_shape=jax.ShapeDtypeStruct(x.shape, x.dtype),
      scratch_shapes=[pltpu.VMEM((TM + _B_HR, width), F32),
                      pltpu.VMEM((_B_HR + SUBLANES, width), F32),
                      pltpu.VMEM((TM, width), F32)],
      compiler_params=pltpu.CompilerParams(
          dimension_semantics=("arbitrary", "arbitrary"),
          vmem_limit_bytes=VMEM_LIMIT_BYTES),
      name="mixer_b",
  )(x, g_pre.reshape(1, d), g_post.reshape(1, d), w_in.astype(BF16),
    w_grp.astype(BF16), scale.reshape(1, width), w_out.astype(BF16))


_C_HR = 2 * SUBLANES


def _mixer_c_kernel(x_ref, gpre_ref, gpost_ref, win_ref, cw_ref, wout_ref,
                    o_ref, ext_scr, prev_scr, *, width):
  hr = _C_HR

  @pl.when(pl.program_id(1) == 0)
  def _():
    prev_scr[...] = jnp.zeros_like(prev_scr)

  x = x_ref[...]
  h = _rms(x, gpre_ref[...]).astype(BF16)
  proj = _dot(h, win_ref[...])
  bg = proj[:, :width]
  m = proj[:, width:2 * width] * proj[:, 2 * width:]
  ext_scr[hr:, :] = m
  _fill_halo(ext_scr, prev_scr, hr)
  cw = cw_ref[...]
  conv = (cw[2:3] * m
          + cw[1:2] * ext_scr[hr - SUBLANES:hr - SUBLANES + TM, :]
          + cw[0:1] * ext_scr[0:TM, :])
  y = _dot((bg * conv).astype(BF16), wout_ref[...])
  o_ref[...] = x + _rms(y, gpost_ref[...])


def _mixer_c_layer(x, g_pre, g_post, w_in, conv_w, w_out):
  bsz, seq, d = x.shape
  width = w_out.shape[0]
  kern = functools.partial(_mixer_c_kernel, width=width)
  full = lambda shape: pl.BlockSpec(shape, lambda b, s: (0,) * len(shape))
  return pl.pallas_call(
      kern,
      grid=(bsz, seq // TM),
      in_specs=[
          pl.BlockSpec((None, TM, d), lambda b, s: (b, s, 0)),
          full((1, d)), full((1, d)),
          full((d, 3 * width)), full((3, width)), full((width, d)),
      ],
      out_specs=pl.BlockSpec((None, TM, d), lambda b, s: (b, s, 0)),
      out_shape=jax.ShapeDtypeStruct(x.shape, x.dtype),
      scratch_shapes=[pltpu.VMEM((TM + _C_HR, width), F32),
                      pltpu.VMEM((_C_HR + SUBLANES, width), F32)],
      compiler_params=pltpu.CompilerParams(
          dimension_semantics=("arbitrary", "arbitrary"),
          vmem_limit_bytes=VMEM_LIMIT_BYTES),
      name="mixer_c",
  )(x, g_pre.reshape(1, d), g_post.reshape(1, d), w_in.astype(BF16), conv_w,
    w_out.astype(BF16))


_D_HR = (D_CONV - 1) * SUBLANES
_D_CB = 256
_PACK = 2 * SUBLANES
_D_PAIRS = 16
_D_BF16_TAPS = 4


def _mixer_d_kernel(x_ref, gpre_ref, gpost_ref, w1_ref, b1_ref, cw_ref, cb_ref,
                    lng_ref, lnb_ref, w2_ref, b2_ref, o_ref, ext_scr, prev_scr,
                    even_scr, odd_scr, conv_scr, *, width):
  hr = _D_HR

  @pl.when(pl.program_id(1) == 0)
  def _():
    prev_scr[...] = jnp.zeros_like(prev_scr)

  x = x_ref[...]
  h = _rms(x, gpre_ref[...]).astype(BF16)

  n_even = (TM + hr) // _PACK
  n_odd = n_even - 1
  for cb in range(width // _D_CB):
    cs = slice(cb * _D_CB, (cb + 1) * _D_CB)
    gs = slice(width + cb * _D_CB, width + (cb + 1) * _D_CB)
    a = _dot(h, w1_ref[:, cs]) + b1_ref[:, cs]
    gate = _dot(h, w1_ref[:, gs]) + b1_ref[:, gs]
    ext_scr[hr:, cs] = a * jax.nn.sigmoid(gate)
    _fill_halo(ext_scr, prev_scr, hr, cs)
    even_scr[:, cs] = ext_scr[0:n_even * _PACK, cs].astype(BF16)
    odd_scr[:, cs] = ext_scr[SUBLANES:SUBLANES + n_odd * _PACK, cs].astype(BF16)
    for lb in range(cb * _D_CB // LANES, (cb + 1) * _D_CB // LANES):
      cols = slice(lb * LANES, (lb + 1) * LANES)
      for m0 in range(0, TM // _PACK, _D_PAIRS):
        accs = [None] * _D_PAIRS
        for k0 in range(0, D_CONV, _D_BF16_TAPS):
          part = [None] * _D_PAIRS
          for k in range(k0, min(k0 + _D_BF16_TAPS, D_CONV)):
            tap = cw_ref[k, :, cols]
            src = even_scr if k % 2 == 0 else odd_scr
            for g in range(_D_PAIRS):
              off = (m0 + g + k // 2) * _PACK
              term = src[off:off + _PACK, cols] * tap
              part[g] = term if part[g] is None else part[g] + term
          for g in range(_D_PAIRS):
            p32 = part[g].astype(F32)
            accs[g] = p32 if accs[g] is None else accs[g] + p32
        for g in range(_D_PAIRS):
          base = (m0 + g) * _PACK
          conv_scr[base:base + _PACK, cols] = accs[g]

  z = _layernorm(conv_scr[...] + cb_ref[...], lng_ref[...], lnb_ref[...])
  z = (z * jax.nn.sigmoid(z)).astype(BF16)
  y = _dot(z, w2_ref[...]) + b2_ref[...]
  o_ref[...] = x + _rms(y, gpost_ref[...])


def _mixer_d_layer(x, g_pre, g_post, w1, b1, conv_w, conv_b, ln_g, ln_b, w2,
                   b2):
  bsz, seq, d = x.shape
  width = w2.shape[0]
  assert conv_w.shape[0] == D_CONV and D_CONV - 1 <= STRIP
  kern = functools.partial(_mixer_d_kernel, width=width)
  full = lambda shape: pl.BlockSpec(shape, lambda b, s: (0,) * len(shape))
  return pl.pallas_call(
      kern,
      grid=(bsz, seq // TM),
      in_specs=[
          pl.BlockSpec((None, TM, d), lambda b, s: (b, s, 0)),
          full((1, d)), full((1, d)),
          full((d, 2 * width)), full((1, 2 * width)),
          full((D_CONV, _PACK, width)),
          full((1, width)), full((1, width)), full((1, width)),
          full((width, d)), full((1, d)),
      ],
      out_specs=pl.BlockSpec((None, TM, d), lambda b, s: (b, s, 0)),
      out_shape=jax.ShapeDtypeStruct(x.shape, x.dtype),
      scratch_shapes=[pltpu.VMEM((TM + _D_HR, width), F32),
                      pltpu.VMEM((_D_HR + SUBLANES, width), F32),
                      pltpu.VMEM((TM + _D_HR, width), BF16),
                      pltpu.VMEM((TM + _D_HR - _PACK, width), BF16),
                      pltpu.VMEM((TM, width), F32)],
      compiler_params=pltpu.CompilerParams(
          dimension_semantics=("arbitrary", "arbitrary"),
          vmem_limit_bytes=VMEM_LIMIT_BYTES),
      name="mixer_d",
  )(x, g_pre.reshape(1, d), g_post.reshape(1, d), w1.astype(BF16),
    b1.reshape(1, -1),
    jnp.broadcast_to(conv_w.astype(BF16)[:, None, :], (D_CONV, _PACK, width)),
    conv_b.reshape(1, width), ln_g.reshape(1, width),
    ln_b.reshape(1, width), w2.astype(BF16), b2.reshape(1, d))


def kernel(x, norm_mix_pre, norm_mix_post, norm_ffn_pre, norm_ffn_post, a_w_in, a_w_s, a_b_s, a_ln_g, a_ln_b, a_w_out, b_w_in, b_w_grp, b_scale, b_w_out, c_w_in, c_conv_w, c_w_out, d_w1, d_b1, d_conv_w, d_conv_b, d_ln_g, d_ln_b, d_w2, d_b2, f_w_up, f_conv_w, f_w_down):
  depth = norm_mix_pre.shape[0]
  strips = False

  def want(flag, x, strips):
    if flag and not strips:
      return _to_strips(x), True
    if strips and not flag:
      return _from_strips(x), False
    return x, strips

  for i in range(depth):
    m, j = i % 4, i // 4
    gp, gq = norm_mix_pre[i], norm_mix_post[i]
    x, strips = want(m != 0, x, strips)
    if m == 0:
      x = _mixer_a_layer(x, gp, gq, a_w_in[j], a_w_s[j], a_b_s[j], a_ln_g[j],
                         a_ln_b[j], a_w_out[j])
    elif m == 1:
      x = _mixer_b_layer(x, gp, gq, b_w_in[j], b_w_grp[j], b_scale[j],
                         b_w_out[j])
    elif m == 2:
      x = _mixer_c_layer(x, gp, gq, c_w_in[j], c_conv_w[j], c_w_out[j])
    else:
      x = _mixer_d_layer(x, gp, gq, d_w1[j], d_b1[j], d_conv_w[j], d_conv_b[j],
                         d_ln_g[j], d_ln_b[j], d_w2[j], d_b2[j])
    x, strips = want(True, x, strips)
    x = _ffn_layer(x, norm_ffn_pre[i], norm_ffn_post[i], f_w_up[i],
                   f_conv_w[i], f_w_down[i])
  x, strips = want(False, x, strips)
  return x
```

```python
import functools

import jax
import jax.numpy as jnp
from jax import lax
from jax.experimental import pallas as pl
from jax.experimental.pallas import tpu as pltpu

EPS = 1e-6
CHUNK = 64
A_BLOCK = 128
A_HEADS = 8
B_WINDOWS = (2, 4, 8, 16)
B_GROUP_DIM = 256
D_CONV = 31

V7X_VMEM_BYTES = 64 * 2**20
VMEM_LIMIT_BYTES = V7X_VMEM_BYTES - 8 * 2**20
SUBLANES = 8
LANES = 128

TM = 512
STRIP = TM // SUBLANES
FFN_FC = 256

F32 = jnp.float32
BF16 = jnp.bfloat16


def _rms(x, g):
  ms = jnp.mean(x * x, axis=-1, keepdims=True)
  return x * lax.rsqrt(ms + EPS) * g


def _layernorm(x, g, b):
  mu = jnp.mean(x, axis=-1, keepdims=True)
  xc = x - mu
  var = jnp.mean(xc * xc, axis=-1, keepdims=True)
  return xc * lax.rsqrt(var + EPS) * g + b


def _dot(a, b):
  return jnp.dot(a, b, preferred_element_type=F32)


def _to_strips(x):
  b, s, d = x.shape
  return (x.reshape(b, s // TM, SUBLANES, STRIP, d).transpose(0, 1, 3, 2, 4)
          .reshape(b, s, d))


def _from_strips(x):
  b, s, d = x.shape
  return (x.reshape(b, s // TM, STRIP, SUBLANES, d).transpose(0, 1, 3, 2, 4)
          .reshape(b, s, d))


def _fill_halo(ext, prev, hr, cols=slice(None)):
  cur_rot = ext[TM - 1:TM - 1 + hr, cols]
  prev_rot = prev[SUBLANES - 1:SUBLANES - 1 + hr, cols]
  sub = lax.broadcasted_iota(jnp.int32, cur_rot.shape, 0) % SUBLANES
  ext[0:hr, cols] = jnp.where(sub == 0, prev_rot, cur_rot)
  prev[0:hr, cols] = ext[TM:TM + hr, cols]


_FFN_HR = 2 * SUBLANES


def _ffn_kernel(x_ref, gpre_ref, gpost_ref, wup_ref, cw_ref, wdn_ref, o_ref,
                h_scr, t_scr, prev_scr, *, fc, nc):
  hr = _FFN_HR
  d_ff = fc * nc

  @pl.when(pl.program_id(1) == 0)
  def _():
    prev_scr[...] = jnp.zeros_like(prev_scr)

  def cols(j):
    return (slice(j * fc, (j + 1) * fc),
            slice(d_ff + j * fc, d_ff + (j + 1) * fc))

  def up(j, slot):
    for half, cs in enumerate(cols(j)):
      t_scr[slot, hr:, half * fc:(half + 1) * fc] = _dot(h_scr[...],
                                                         wup_ref[:, cs])

  def proc(j, slot):
    ext = t_scr.at[slot]
    _fill_halo(ext, prev_scr.at[j], hr)
    halves = []
    for half, cs in enumerate(cols(j)):
      lanes = slice(half * fc, (half + 1) * fc)
      halves.append(cw_ref[2:3, cs] * ext[hr:hr + TM, lanes]
                    + cw_ref[1:2, cs] * ext[hr - SUBLANES:hr - SUBLANES + TM, lanes]
                    + cw_ref[0:1, cs] * ext[0:TM, lanes])
    g, u = halves
    a = (g * jax.nn.sigmoid(g) * u).astype(BF16)
    return _dot(a, wdn_ref[j * fc:(j + 1) * fc, :])

  x = x_ref[...]
  h_scr[...] = _rms(x, gpre_ref[...]).astype(BF16)
  up(0, 0)
  y = None
  for j in range(nc):
    if j + 1 < nc:
      up(j + 1, (j + 1) % 2)
    yj = proc(j, j % 2)
    y = yj if y is None else y + yj
  o_ref[...] = x + _rms(y, gpost_ref[...])


def _ffn_layer(x, layer, g_pre, g_post, w_up, conv_w, w_down):
  bsz, seq, d = x.shape
  d_ff = w_down.shape[1]
  fc = FFN_FC
  nc = d_ff // fc
  assert nc * fc == d_ff and seq % TM == 0
  kern = functools.partial(_ffn_kernel, fc=fc, nc=nc)
  full = lambda shape: pl.BlockSpec(shape, lambda b, s: (0,) * len(shape))
  of_layer = lambda shape: pl.BlockSpec(
      (None,) + shape, lambda b, s: (layer,) + (0,) * len(shape),
      pipeline_mode=pl.Buffered(1))
  return pl.pallas_call(
      kern,
      grid=(bsz, seq // TM),
      in_specs=[
          pl.BlockSpec((None, TM, d), lambda b, s: (b, s, 0)),
          full((1, d)), full((1, d)),
          of_layer((d, 2 * d_ff)), of_layer((3, 2 * d_ff)), of_layer((d_ff, d)),
      ],
      out_specs=pl.BlockSpec((None, TM, d), lambda b, s: (b, s, 0)),
      out_shape=jax.ShapeDtypeStruct(x.shape, x.dtype),
      scratch_shapes=[
          pltpu.VMEM((TM, d), BF16),
          pltpu.VMEM((2, TM + _FFN_HR, 2 * fc), F32),
          pltpu.VMEM((nc, _FFN_HR + SUBLANES, 2 * fc), F32),
      ],
      compiler_params=pltpu.CompilerParams(
          dimension_semantics=("arbitrary", "arbitrary"),
          vmem_limit_bytes=VMEM_LIMIT_BYTES),
      name="ffn",
  )(x, g_pre.reshape(1, d), g_post.reshape(1, d), w_up, conv_w, w_down)


def _mixer_a_kernel(x_ref, gpre_ref, gpost_ref, win_ref, ws_ref, bias_ref,
                    lng_ref, lnb_ref, wout_ref, o_ref, z_scr, *, width):
  q = lax.broadcasted_iota(jnp.int32, (A_BLOCK, A_BLOCK), 0) // CHUNK
  p = lax.broadcasted_iota(jnp.int32, (A_BLOCK, A_BLOCK), 1) // CHUNK
  mask = p <= q
  hd = width // A_HEADS
  wms = [jnp.where(mask, ws_ref[g], 0.0).astype(BF16) for g in range(A_HEADS)]

  def project(rows):
    x = x_ref[rows, :]
    h = _rms(x, gpre_ref[...]).astype(BF16)
    uv = _dot(h, win_ref[...])
    uv = 0.5 * uv * (1.0 + lax.erf(uv * (2.0 ** -0.5)))
    v = _layernorm(uv[:, width:], lng_ref[...], lnb_ref[...]).astype(BF16)
    return x, uv[:, :width], v

  def gate_and_mix(rows, x, u, v):
    for g in range(A_HEADS):
      cols = slice(g * hd, (g + 1) * hd)
      vv = jnp.concatenate([v[:A_BLOCK, cols], v[A_BLOCK:, cols]], axis=1)
      zz = _dot(wms[g], vv)
      bias = bias_ref[:, cols]
      z_scr[rows.start:rows.start + A_BLOCK, cols] = zz[:, :hd] + bias
      z_scr[rows.start + A_BLOCK:rows.stop, cols] = zz[:, hd:] + bias
    y = _dot((u * z_scr[rows, :]).astype(BF16), wout_ref[...])
    o_ref[rows, :] = x + _rms(y, gpost_ref[...])

  pairs = [slice(r, r + 2 * A_BLOCK) for r in range(0, TM, 2 * A_BLOCK)]
  state = project(pairs[0])
  for i, rows in enumerate(pairs):
    nxt = project(pairs[i + 1]) if i + 1 < len(pairs) else None
    gate_and_mix(rows, *state)
    state = nxt


def _mixer_a_layer(x, g_pre, g_post, w_in, w_s, b_s, ln_g, ln_b, w_out):
  bsz, seq, d = x.shape
  width = w_out.shape[0]
  hd = width // A_HEADS
  bias = jnp.repeat(b_s.T, hd, axis=1)
  kern = functools.partial(_mixer_a_kernel, width=width)
  full = lambda shape: pl.BlockSpec(shape, lambda b, s: (0,) * len(shape))
  return pl.pallas_call(
      kern,
      grid=(bsz, seq // TM),
      in_specs=[
          pl.BlockSpec((None, TM, d), lambda b, s: (b, s, 0)),
          full((1, d)), full((1, d)),
          full((d, 2 * width)), full((A_HEADS, A_BLOCK, A_BLOCK)),
          full((A_BLOCK, width)), full((1, width)), full((1, width)),
          full((width, d)),
      ],
      out_specs=pl.BlockSpec((None, TM, d), lambda b, s: (b, s, 0)),
      out_shape=jax.ShapeDtypeStruct(x.shape, x.dtype),
      scratch_shapes=[pltpu.VMEM((TM, width), F32)],
      compiler_params=pltpu.CompilerParams(
          dimension_semantics=("arbitrary", "arbitrary"),
          vmem_limit_bytes=VMEM_LIMIT_BYTES),
      name="mixer_a",
  )(x, g_pre.reshape(1, d), g_post.reshape(1, d), w_in.astype(BF16), w_s, bias,
    ln_g.reshape(1, width), ln_b.reshape(1, width), w_out.astype(BF16))


_B_HR = (max(B_WINDOWS) - 1) * SUBLANES


def _mixer_b_kernel(x_ref, gpre_ref, gpost_ref, win_ref, wgrp_ref, scale_ref,
                    wout_ref, o_ref, ext_scr, prev_scr, mix_scr):
  hr = _B_HR
  s = pl.program_id(1)

  @pl.when(s == 0)
  def _():
    prev_scr[...] = jnp.zeros_like(prev_scr)

  x = x_ref[...]
  h = _rms(x, gpre_ref[...]).astype(BF16)
  ext_scr[hr:, :] = _dot(h, win_ref[...])
  _fill_halo(ext_scr, prev_scr, hr)
  gd = B_GROUP_DIM
  row = lax.broadcasted_iota(jnp.int32, (TM, gd), 0)
  pos = s * TM + (row % SUBLANES) * STRIP + row // SUBLANES + 1
  for g, w in enumerate(B_WINDOWS):
    cols = slice(g * gd, (g + 1) * gd)
    p = ext_scr[hr:, cols]
    win = p
    for k in range(1, w):
      win = win + ext_scr[hr - k * SUBLANES:hr - k * SUBLANES + TM, cols]
    cnt = jnp.minimum(pos, w).astype(F32)
    pooled = (win / cnt - p).astype(BF16)
    mix_scr[:, cols] = _dot(pooled, wgrp_ref[g])
  y = _dot((mix_scr[...] * scale_ref[...]).astype(BF16), wout_ref[...])
  o_ref[...] = x + _rms(y, gpost_ref[...])


def _mixer_b_layer(x, g_pre, g_post, w_in, w_grp, scale, w_out):
  bsz, seq, d = x.shape
  width = w_out.shape[0]
  ng, gd, _ = w_grp.shape
  assert gd == B_GROUP_DIM and ng == len(B_WINDOWS)
  full = lambda shape: pl.BlockSpec(shape, lambda b, s: (0,) * len(shape))
  return pl.pallas_call(
      _mixer_b_kernel,
      grid=(bsz, seq // TM),
      in_specs=[
          pl.BlockSpec((None, TM, d), lambda b, s: (b, s, 0)),
          full((1, d)), full((1, d)),
          full((d, width)), full((ng, gd, gd)), full((1, width)),
          full((width, d)),
      ],
      out_specs=pl.BlockSpec((None, TM, d), lambda b, s: (b, s, 0)),
      out_shape=jax.ShapeDtypeStruct(x.shape, x.dtype),
      scratch_shapes=[pltpu.VMEM((TM + _B_HR, width), F32),
                      pltpu.VMEM((_B_HR + SUBLANES, width), F32),
                      pltpu.VMEM((TM, width), F32)],
      compiler_params=pltpu.CompilerParams(
          dimension_semantics=("arbitrary", "arbitrary"),
          vmem_limit_bytes=VMEM_LIMIT_BYTES),
      name="mixer_b",
  )(x, g_pre.reshape(1, d), g_post.reshape(1, d), w_in.astype(BF16),
    w_grp.astype(BF16), scale.reshape(1, width), w_out.astype(BF16))


_C_HR = 2 * SUBLANES


def _mixer_c_kernel(x_ref, gpre_ref, gpost_ref, win_ref, cw_ref, wout_ref,
                    o_ref, ext_scr, prev_scr, *, width):
  hr = _C_HR

  @pl.when(pl.program_id(1) == 0)
  def _():
    prev_scr[...] = jnp.zeros_like(prev_scr)

  x = x_ref[...]
  h = _rms(x, gpre_ref[...]).astype(BF16)
  proj = _dot(h, win_ref[...])
  bg = proj[:, :width]
  m = proj[:, width:2 * width] * proj[:, 2 * width:]
  ext_scr[hr:, :] = m
  _fill_halo(ext_scr, prev_scr, hr)
  cw = cw_ref[...]
  conv = (cw[2:3] * m
          + cw[1:2] * ext_scr[hr - SUBLANES:hr - SUBLANES + TM, :]
          + cw[0:1] * ext_scr[0:TM, :])
  y = _dot((bg * conv).astype(BF16), wout_ref[...])
  o_ref[...] = x + _rms(y, gpost_ref[...])


def _mixer_c_layer(x, g_pre, g_post, w_in, conv_w, w_out):
  bsz, seq, d = x.shape
  width = w_out.shape[0]
  kern = functools.partial(_mixer_c_kernel, width=width)
  full = lambda shape: pl.BlockSpec(shape, lambda b, s: (0,) * len(shape))
  return pl.pallas_call(
      kern,
      grid=(bsz, seq // TM),
      in_specs=[
          pl.BlockSpec((None, TM, d), lambda b, s: (b, s, 0)),
          full((1, d)), full((1, d)),
          full((d, 3 * width)), full((3, width)), full((width, d)),
      ],
      out_specs=pl.BlockSpec((None, TM, d), lambda b, s: (b, s, 0)),
      out_shape=jax.ShapeDtypeStruct(x.shape, x.dtype),
      scratch_shapes=[pltpu.VMEM((TM + _C_HR, width), F32),
                      pltpu.VMEM((_C_HR + SUBLANES, width), F32)],
      compiler_params=pltpu.CompilerParams(
          dimension_semantics=("arbitrary", "arbitrary"),
          vmem_limit_bytes=VMEM_LIMIT_BYTES),
      name="mixer_c",
  )(x, g_pre.reshape(1, d), g_post.reshape(1, d), w_in.astype(BF16), conv_w,
    w_out.astype(BF16))


_D_HR = (D_CONV - 1) * SUBLANES
_D_CB = 256
_D_ROWS = 16


def _mixer_d_kernel(x_ref, gpre_ref, gpost_ref, w1_ref, b1_ref, cw_ref, cb_ref,
                    lng_ref, lnb_ref, w2_ref, b2_ref, o_ref, ext_scr, prev_scr,
                    conv_scr, *, width):
  hr = _D_HR

  @pl.when(pl.program_id(1) == 0)
  def _():
    prev_scr[...] = jnp.zeros_like(prev_scr)

  x = x_ref[...]
  h = _rms(x, gpre_ref[...]).astype(BF16)

  order_bits = None
  for cb in range(width // _D_CB):
    cs = slice(cb * _D_CB, (cb + 1) * _D_CB)
    gs = slice(width + cb * _D_CB, width + (cb + 1) * _D_CB)
    a = _dot(h, w1_ref[:, cs]) + b1_ref[:, cs]
    gate = _dot(h, w1_ref[:, gs]) + b1_ref[:, gs]
    ext_scr[hr:, cs] = a * jax.nn.sigmoid(gate)
    _fill_halo(ext_scr, prev_scr, hr, cs)
    for lb in range(cb * _D_CB // LANES, (cb + 1) * _D_CB // LANES):
      cols = slice(lb * LANES, (lb + 1) * LANES)
      for i0 in range(0, STRIP, _D_ROWS):
        accs = [None] * _D_ROWS
        for k in range(D_CONV - 1, -1, -1):
          tap = cw_ref[k, :, cols]
          if order_bits is not None:
            tap = pltpu.bitcast(pltpu.bitcast(tap, jnp.uint32) | order_bits, F32)
          for g in range(_D_ROWS):
            off = (i0 + g + k) * SUBLANES
            term = tap * ext_scr[off:off + SUBLANES, cols]
            accs[g] = term if accs[g] is None else accs[g] + term
        for g in range(_D_ROWS):
          base = (i0 + g) * SUBLANES
          conv_scr[base:base + SUBLANES, cols] = accs[g]
        order_bits = (pltpu.bitcast(accs[0], jnp.uint32) >> 16) >> 16

  z = _layernorm(conv_scr[...] + cb_ref[...], lng_ref[...], lnb_ref[...])
  z = (z * jax.nn.sigmoid(z)).astype(BF16)
  y = _dot(z, w2_ref[...]) + b2_ref[...]
  o_ref[...] = x + _rms(y, gpost_ref[...])


def _mixer_d_layer(x, g_pre, g_post, w1, b1, conv_w, conv_b, ln_g, ln_b, w2,
                   b2):
  bsz, seq, d = x.shape
  width = w2.shape[0]
  assert conv_w.shape[0] == D_CONV and D_CONV - 1 <= STRIP
  kern = functools.partial(_mixer_d_kernel, width=width)
  full = lambda shape: pl.BlockSpec(shape, lambda b, s: (0,) * len(shape))
  return pl.pallas_call(
      kern,
      grid=(bsz, seq // TM),
      in_specs=[
          pl.BlockSpec((None, TM, d), lambda b, s: (b, s, 0)),
          full((1, d)), full((1, d)),
          full((d, 2 * width)), full((1, 2 * width)),
          full((D_CONV, SUBLANES, width)),
          full((1, width)), full((1, width)), full((1, width)),
          full((width, d)), full((1, d)),
      ],
      out_specs=pl.BlockSpec((None, TM, d), lambda b, s: (b, s, 0)),
      out_shape=jax.ShapeDtypeStruct(x.shape, x.dtype),
      scratch_shapes=[pltpu.VMEM((TM + _D_HR, width), F32),
                      pltpu.VMEM((_D_HR + SUBLANES, width), F32),
                      pltpu.VMEM((TM, width), F32)],
      compiler_params=pltpu.CompilerParams(
          dimension_semantics=("arbitrary", "arbitrary"),
          vmem_limit_bytes=VMEM_LIMIT_BYTES),
      name="mixer_d",
  )(x, g_pre.reshape(1, d), g_post.reshape(1, d), w1.astype(BF16),
    b1.reshape(1, -1),
    jnp.broadcast_to(conv_w[:, None, :], (D_CONV, SUBLANES, width)),
    conv_b.reshape(1, width), ln_g.reshape(1, width),
    ln_b.reshape(1, width), w2.astype(BF16), b2.reshape(1, d))


def kernel(x, norm_mix_pre, norm_mix_post, norm_ffn_pre, norm_ffn_post, a_w_in, a_w_s, a_b_s, a_ln_g, a_ln_b, a_w_out, b_w_in, b_w_grp, b_scale, b_w_out, c_w_in, c_conv_w, c_w_out, d_w1, d_b1, d_conv_w, d_conv_b, d_ln_g, d_ln_b, d_w2, d_b2, f_w_up, f_conv_w, f_w_down):
  depth = norm_mix_pre.shape[0]
  f_w_up_bf16 = f_w_up.astype(BF16)
  f_w_down_bf16 = f_w_down.astype(BF16)
  strips = False

  def want(flag, x, strips):
    if flag and not strips:
      return _to_strips(x), True
    if strips and not flag:
      return _from_strips(x), False
    return x, strips

  for i in range(depth):
    m, j = i % 4, i // 4
    gp, gq = norm_mix_pre[i], norm_mix_post[i]
    x, strips = want(m != 0, x, strips)
    if m == 0:
      x = _mixer_a_layer(x, gp, gq, a_w_in[j], a_w_s[j], a_b_s[j], a_ln_g[j],
                         a_ln_b[j], a_w_out[j])
    elif m == 1:
      x = _mixer_b_layer(x, gp, gq, b_w_in[j], b_w_grp[j], b_scale[j],
                         b_w_out[j])
    elif m == 2:
      x = _mixer_c_layer(x, gp, gq, c_w_in[j], c_conv_w[j], c_w_out[j])
    else:
      x = _mixer_d_layer(x, gp, gq, d_w1[j], d_b1[j], d_conv_w[j], d_conv_b[j],
                         d_ln_g[j], d_ln_b[j], d_w2[j], d_b2[j])
    x, strips = want(True, x, strips)
    x = _ffn_layer(x, i, norm_ffn_pre[i], norm_ffn_post[i], f_w_up_bf16,
                   f_conv_w, f_w_down_bf16)
  x, strips = want(False, x, strips)
  return x
```

```python
import functools

import jax
import jax.numpy as jnp
from jax import lax
from jax.experimental import pallas as pl
from jax.experimental.pallas import tpu as pltpu

EPS = 1e-6
CHUNK = 64
A_BLOCK = 128
A_HEADS = 8
B_WINDOWS = (2, 4, 8, 16)
B_GROUP_DIM = 256
D_CONV = 31

V7X_VMEM_BYTES = 64 * 2**20
VMEM_LIMIT_BYTES = V7X_VMEM_BYTES - 8 * 2**20
SUBLANES = 8
LANES = 128

TM = 512
STRIP = TM // SUBLANES
FFN_FC = 256

F32 = jnp.float32
BF16 = jnp.bfloat16


def _rms(x, g):
  ms = jnp.mean(x * x, axis=-1, keepdims=True)
  return x * lax.rsqrt(ms + EPS) * g


def _layernorm(x, g, b):
  mu = jnp.mean(x, axis=-1, keepdims=True)
  xc = x - mu
  var = jnp.mean(xc * xc, axis=-1, keepdims=True)
  return xc * lax.rsqrt(var + EPS) * g + b


def _dot(a, b):
  return jnp.dot(a, b, preferred_element_type=F32)


def _to_strips(x):
  b, s, d = x.shape
  return (x.reshape(b, s // TM, SUBLANES, STRIP, d).transpose(0, 1, 3, 2, 4)
          .reshape(b, s, d))


def _from_strips(x):
  b, s, d = x.shape
  return (x.reshape(b, s // TM, STRIP, SUBLANES, d).transpose(0, 1, 3, 2, 4)
          .reshape(b, s, d))


_CAST_BLOCK_BYTES = 8 * 2**20


def _cast_kernel(w_ref, o_ref):
  o_ref[...] = w_ref[...].astype(o_ref.dtype)


def _to_bf16(w):
  n, rows, cols = w.shape
  pack = 2 * SUBLANES
  fits = [r for r in range(pack, rows + 1, pack)
          if rows % r == 0 and r * cols * 4 <= _CAST_BLOCK_BYTES]
  rb = max(fits)
  spec = pl.BlockSpec((None, rb, cols), lambda l, r: (l, r, 0))
  return pl.pallas_call(
      _cast_kernel,
      grid=(n, rows // rb),
      in_specs=[spec],
      out_specs=spec,
      out_shape=jax.ShapeDtypeStruct(w.shape, BF16),
      compiler_params=pltpu.CompilerParams(
          dimension_semantics=("arbitrary", "arbitrary"),
          vmem_limit_bytes=VMEM_LIMIT_BYTES),
      name="to_bf16",
  )(w)


def _fill_halo(ext, prev, hr, cols=slice(None)):
  cur_rot = ext[TM - 1:TM - 1 + hr, cols]
  prev_rot = prev[SUBLANES - 1:SUBLANES - 1 + hr, cols]
  sub = lax.broadcasted_iota(jnp.int32, cur_rot.shape, 0) % SUBLANES
  ext[0:hr, cols] = jnp.where(sub == 0, prev_rot, cur_rot)
  prev[0:hr, cols] = ext[TM:TM + hr, cols]


_FFN_HR = 2 * SUBLANES


def _ffn_kernel(x_ref, gpre_ref, gpost_ref, wup_ref, cw_ref, wdn_ref, o_ref,
                h_scr, t_scr, prev_scr, *, fc, nc):
  hr = _FFN_HR
  d_ff = fc * nc

  @pl.when(pl.program_id(1) == 0)
  def _():
    prev_scr[...] = jnp.zeros_like(prev_scr)

  def cols(j):
    return (slice(j * fc, (j + 1) * fc),
            slice(d_ff + j * fc, d_ff + (j + 1) * fc))

  def up(j, slot):
    for half, cs in enumerate(cols(j)):
      t_scr[slot, hr:, half * fc:(half + 1) * fc] = _dot(h_scr[...],
                                                         wup_ref[:, cs])

  def proc(j, slot):
    ext = t_scr.at[slot]
    _fill_halo(ext, prev_scr.at[j], hr)
    halves = []
    for half, cs in enumerate(cols(j)):
      lanes = slice(half * fc, (half + 1) * fc)
      halves.append(cw_ref[2:3, cs] * ext[hr:hr + TM, lanes]
                    + cw_ref[1:2, cs] * ext[hr - SUBLANES:hr - SUBLANES + TM, lanes]
                    + cw_ref[0:1, cs] * ext[0:TM, lanes])
    g, u = halves
    a = (g * jax.nn.sigmoid(g) * u).astype(BF16)
    return _dot(a, wdn_ref[j * fc:(j + 1) * fc, :])

  x = x_ref[...]
  h_scr[...] = _rms(x, gpre_ref[...]).astype(BF16)
  up(0, 0)
  y = None
  for j in range(nc):
    if j + 1 < nc:
      up(j + 1, (j + 1) % 2)
    yj = proc(j, j % 2)
    y = yj if y is None else y + yj
  o_ref[...] = x + _rms(y, gpost_ref[...])


def _ffn_layer(x, layer, g_pre, g_post, w_up, conv_w, w_down):
  bsz, seq, d = x.shape
  d_ff = w_down.shape[1]
  fc = FFN_FC
  nc = d_ff // fc
  assert nc * fc == d_ff and seq % TM == 0
  kern = functools.partial(_ffn_kernel, fc=fc, nc=nc)
  full = lambda shape: pl.BlockSpec(shape, lambda b, s: (0,) * len(shape))
  of_layer = lambda shape: pl.BlockSpec(
      (None,) + shape, lambda b, s: (layer,) + (0,) * len(shape),
      pipeline_mode=pl.Buffered(1))
  return pl.pallas_call(
      kern,
      grid=(bsz, seq // TM),
      in_specs=[
          pl.BlockSpec((None, TM, d), lambda b, s: (b, s, 0)),
          full((1, d)), full((1, d)),
          of_layer((d, 2 * d_ff)), of_layer((3, 2 * d_ff)), of_layer((d_ff, d)),
      ],
      out_specs=pl.BlockSpec((None, TM, d), lambda b, s: (b, s, 0)),
      out_shape=jax.ShapeDtypeStruct(x.shape, x.dtype),
      scratch_shapes=[
          pltpu.VMEM((TM, d), BF16),
          pltpu.VMEM((2, TM + _FFN_HR, 2 * fc), F32),
          pltpu.VMEM((nc, _FFN_HR + SUBLANES, 2 * fc), F32),
      ],
      compiler_params=pltpu.CompilerParams(
          dimension_semantics=("arbitrary", "arbitrary"),
          vmem_limit_bytes=VMEM_LIMIT_BYTES),
      name="ffn",
  )(x, g_pre.reshape(1, d), g_post.reshape(1, d), w_up, conv_w, w_down)


def _mixer_a_kernel(x_ref, gpre_ref, gpost_ref, win_ref, ws_ref, bias_ref,
                    lng_ref, lnb_ref, wout_ref, o_ref, z_scr, *, width):
  q = lax.broadcasted_iota(jnp.int32, (A_BLOCK, A_BLOCK), 0) // CHUNK
  p = lax.broadcasted_iota(jnp.int32, (A_BLOCK, A_BLOCK), 1) // CHUNK
  mask = p <= q
  hd = width // A_HEADS
  wms = [jnp.where(mask, ws_ref[g], 0.0).astype(BF16) for g in range(A_HEADS)]

  def project(rows):
    x = x_ref[rows, :]
    h = _rms(x, gpre_ref[...]).astype(BF16)
    uv = _dot(h, win_ref[...])
    uv = 0.5 * uv * (1.0 + lax.erf(uv * (2.0 ** -0.5)))
    v = _layernorm(uv[:, width:], lng_ref[...], lnb_ref[...]).astype(BF16)
    return x, uv[:, :width], v

  def gate_and_mix(rows, x, u, v):
    for g in range(A_HEADS):
      cols = slice(g * hd, (g + 1) * hd)
      vv = jnp.concatenate([v[:A_BLOCK, cols], v[A_BLOCK:, cols]], axis=1)
      zz = _dot(wms[g], vv)
      bias = bias_ref[:, cols]
      z_scr[rows.start:rows.start + A_BLOCK, cols] = zz[:, :hd] + bias
      z_scr[rows.start + A_BLOCK:rows.stop, cols] = zz[:, hd:] + bias
    y = _dot((u * z_scr[rows, :]).astype(BF16), wout_ref[...])
    out = x + _rms(y, gpost_ref[...])
    d = out.shape[-1]
    for k in range((rows.stop - rows.start) // STRIP):
      s = rows.start // STRIP + k
      o_ref[:, s * d:(s + 1) * d] = out[k * STRIP:(k + 1) * STRIP, :]

  pairs = [slice(r, r + 2 * A_BLOCK) for r in range(0, TM, 2 * A_BLOCK)]
  state = project(pairs[0])
  for i, rows in enumerate(pairs):
    nxt = project(pairs[i + 1]) if i + 1 < len(pairs) else None
    gate_and_mix(rows, *state)
    state = nxt


def _mixer_a_layer(x, g_pre, g_post, w_in, w_s, b_s, ln_g, ln_b, w_out):
  bsz, seq, d = x.shape
  width = w_out.shape[0]
  hd = width // A_HEADS
  bias = jnp.repeat(b_s.T, hd, axis=1)
  kern = functools.partial(_mixer_a_kernel, width=width)
  full = lambda shape: pl.BlockSpec(shape, lambda b, s: (0,) * len(shape))
  return pl.pallas_call(
      kern,
      grid=(bsz, seq // TM),
      in_specs=[
          pl.BlockSpec((None, TM, d), lambda b, s: (b, s, 0)),
          full((1, d)), full((1, d)),
          full((d, 2 * width)), full((A_HEADS, A_BLOCK, A_BLOCK)),
          full((A_BLOCK, width)), full((1, width)), full((1, width)),
          full((width, d)),
      ],
      out_specs=pl.BlockSpec((None, None, STRIP, SUBLANES * d),
                             lambda b, s: (b, s, 0, 0)),
      out_shape=jax.ShapeDtypeStruct(
          (bsz, seq // TM, STRIP, SUBLANES * d), x.dtype),
      scratch_shapes=[pltpu.VMEM((TM, width), F32)],
      compiler_params=pltpu.CompilerParams(
          dimension_semantics=("arbitrary", "arbitrary"),
          vmem_limit_bytes=VMEM_LIMIT_BYTES),
      name="mixer_a",
  )(x, g_pre.reshape(1, d), g_post.reshape(1, d), w_in.astype(BF16), w_s, bias,
    ln_g.reshape(1, width), ln_b.reshape(1, width), w_out.astype(BF16)
    ).reshape(bsz, seq, d)


_B_HR = (max(B_WINDOWS) - 1) * SUBLANES


def _mixer_b_kernel(x_ref, gpre_ref, gpost_ref, win_ref, wgrp_ref, scale_ref,
                    wout_ref, o_ref, ext_scr, prev_scr, mix_scr):
  hr = _B_HR
  s = pl.program_id(1)

  @pl.when(s == 0)
  def _():
    prev_scr[...] = jnp.zeros_like(prev_scr)

  x = x_ref[...]
  h = _rms(x, gpre_ref[...]).astype(BF16)
  ext_scr[hr:, :] = _dot(h, win_ref[...])
  _fill_halo(ext_scr, prev_scr, hr)
  gd = B_GROUP_DIM
  row = lax.broadcasted_iota(jnp.int32, (TM, gd), 0)
  pos = s * TM + (row % SUBLANES) * STRIP + row // SUBLANES + 1
  for g, w in enumerate(B_WINDOWS):
    cols = slice(g * gd, (g + 1) * gd)
    p = ext_scr[hr:, cols]
    win = p
    for k in range(1, w):
      win = win + ext_scr[hr - k * SUBLANES:hr - k * SUBLANES + TM, cols]
    cnt = jnp.minimum(pos, w).astype(F32)
    pooled = (win / cnt - p).astype(BF16)
    mix_scr[:, cols] = _dot(pooled, wgrp_ref[g])
  y = _dot((mix_scr[...] * scale_ref[...]).astype(BF16), wout_ref[...])
  o_ref[...] = x + _rms(y, gpost_ref[...])


def _mixer_b_layer(x, g_pre, g_post, w_in, w_grp, scale, w_out):
  bsz, seq, d = x.shape
  width = w_out.shape[0]
  ng, gd, _ = w_grp.shape
  assert gd == B_GROUP_DIM and ng == len(B_WINDOWS)
  full = lambda shape: pl.BlockSpec(shape, lambda b, s: (0,) * len(shape))
  return pl.pallas_call(
      _mixer_b_kernel,
      grid=(bsz, seq // TM),
      in_specs=[
          pl.BlockSpec((None, TM, d), lambda b, s: (b, s, 0)),
          full((1, d)), full((1, d)),
          full((d, width)), full((ng, gd, gd)), full((1, width)),
          full((width, d)),
      ],
      out_specs=pl.BlockSpec((None, TM, d), lambda b, s: (b, s, 0)),
      out_shape=jax.ShapeDtypeStruct(x.shape, x.dtype),
      scratch_shapes=[pltpu.VMEM((TM + _B_HR, width), F32),
                      pltpu.VMEM((_B_HR + SUBLANES, width), F32),
                      pltpu.VMEM((TM, width), F32)],
      compiler_params=pltpu.CompilerParams(
          dimension_semantics=("arbitrary", "arbitrary"),
          vmem_limit_bytes=VMEM_LIMIT_BYTES),
      name="mixer_b",
  )(x, g_pre.reshape(1, d), g_post.reshape(1, d), w_in.astype(BF16),
    w_grp.astype(BF16), scale.reshape(1, width), w_out.astype(BF16))


_C_HR = 2 * SUBLANES


def _mixer_c_kernel(x_ref, gpre_ref, gpost_ref, win_ref, cw_ref, wout_ref,
                    o_ref, ext_scr, prev_scr, *, width):
  hr = _C_HR

  @pl.when(pl.program_id(1) == 0)
  def _():
    prev_scr[...] = jnp.zeros_like(prev_scr)

  x = x_ref[...]
  h = _rms(x, gpre_ref[...]).astype(BF16)
  proj = _dot(h, win_ref[...])
  bg = proj[:, :width]
  m = proj[:, width:2 * width] * proj[:, 2 * width:]
  ext_scr[hr:, :] = m
  _fill_halo(ext_scr, prev_scr, hr)
  cw = cw_ref[...]
  conv = (cw[2:3] * m
          + cw[1:2] * ext_scr[hr - SUBLANES:hr - SUBLANES + TM, :]
          + cw[0:1] * ext_scr[0:TM, :])
  y = _dot((bg * conv).astype(BF16), wout_ref[...])
  o_ref[...] = x + _rms(y, gpost_ref[...])


def _mixer_c_layer(x, g_pre, g_post, w_in, conv_w, w_out):
  bsz, seq, d = x.shape
  width = w_out.shape[0]
  kern = functools.partial(_mixer_c_kernel, width=width)
  full = lambda shape: pl.BlockSpec(shape, lambda b, s: (0,) * len(shape))
  return pl.pallas_call(
      kern,
      grid=(bsz, seq // TM),
      in_specs=[
          pl.BlockSpec((None, TM, d), lambda b, s: (b, s, 0)),
          full((1, d)), full((1, d)),
          full((d, 3 * width)), full((3, width)), full((width, d)),
      ],
      out_specs=pl.BlockSpec((None, TM, d), lambda b, s: (b, s, 0)),
      out_shape=jax.ShapeDtypeStruct(x.shape, x.dtype),
      scratch_shapes=[pltpu.VMEM((TM + _C_HR, width), F32),
                      pltpu.VMEM((_C_HR + SUBLANES, width), F32)],
      compiler_params=pltpu.CompilerParams(
          dimension_semantics=("arbitrary", "arbitrary"),
          vmem_limit_bytes=VMEM_LIMIT_BYTES),
      name="mixer_c",
  )(x, g_pre.reshape(1, d), g_post.reshape(1, d), w_in.astype(BF16), conv_w,
    w_out.astype(BF16))


_D_HR = (D_CONV - 1) * SUBLANES
_D_CB = 256
_D_ROWS = 16


def _mixer_d_kernel(x_ref, gpre_ref, gpost_ref, w1_ref, b1_ref, cw_ref, cb_ref,
                    lng_ref, lnb_ref, w2_ref, b2_ref, o_ref, ext_scr, prev_scr,
                    conv_scr, *, width):
  hr = _D_HR

  @pl.when(pl.program_id(1) == 0)
  def _():
    prev_scr[...] = jnp.zeros_like(prev_scr)

  x = x_ref[...]
  h = _rms(x, gpre_ref[...]).astype(BF16)

  order_bits = None
  for cb in range(width // _D_CB):
    cs = slice(cb * _D_CB, (cb + 1) * _D_CB)
    gs = slice(width + cb * _D_CB, width + (cb + 1) * _D_CB)
    a = _dot(h, w1_ref[:, cs]) + b1_ref[:, cs]
    gate = _dot(h, w1_ref[:, gs]) + b1_ref[:, gs]
    ext_scr[hr:, cs] = a * jax.nn.sigmoid(gate)
    _fill_halo(ext_scr, prev_scr, hr, cs)
    for lb in range(cb * _D_CB // LANES, (cb + 1) * _D_CB // LANES):
      cols = slice(lb * LANES, (lb + 1) * LANES)
      for i0 in range(0, STRIP, _D_ROWS):
        accs = [None] * _D_ROWS
        for k in range(D_CONV - 1, -1, -1):
          tap = cw_ref[k, :, cols]
          if order_bits is not None:
            tap = pltpu.bitcast(pltpu.bitcast(tap, jnp.uint32) | order_bits, F32)
          for g in range(_D_ROWS):
            off = (i0 + g + k) * SUBLANES
            term = tap * ext_scr[off:off + SUBLANES, cols]
            accs[g] = term if accs[g] is None else accs[g] + term
        for g in range(_D_ROWS):
          base = (i0 + g) * SUBLANES
          conv_scr[base:base + SUBLANES, cols] = accs[g]
        order_bits = (pltpu.bitcast(accs[0], jnp.uint32) >> 16) >> 16

  z = _layernorm(conv_scr[...] + cb_ref[...], lng_ref[...], lnb_ref[...])
  z = (z * jax.nn.sigmoid(z)).astype(BF16)
  y = _dot(z, w2_ref[...]) + b2_ref[...]
  o_ref[...] = x + _rms(y, gpost_ref[...])


def _mixer_d_layer(x, g_pre, g_post, w1, b1, conv_w, conv_b, ln_g, ln_b, w2,
                   b2):
  bsz, seq, d = x.shape
  width = w2.shape[0]
  assert conv_w.shape[0] == D_CONV and D_CONV - 1 <= STRIP
  kern = functools.partial(_mixer_d_kernel, width=width)
  full = lambda shape: pl.BlockSpec(shape, lambda b, s: (0,) * len(shape))
  return pl.pallas_call(
      kern,
      grid=(bsz, seq // TM),
      in_specs=[
          pl.BlockSpec((None, TM, d), lambda b, s: (b, s, 0)),
          full((1, d)), full((1, d)),
          full((d, 2 * width)), full((1, 2 * width)),
          full((D_CONV, SUBLANES, width)),
          full((1, width)), full((1, width)), full((1, width)),
          full((width, d)), full((1, d)),
      ],
      out_specs=pl.BlockSpec((None, TM, d), lambda b, s: (b, s, 0)),
      out_shape=jax.ShapeDtypeStruct(x.shape, x.dtype),
      scratch_shapes=[pltpu.VMEM((TM + _D_HR, width), F32),
                      pltpu.VMEM((_D_HR + SUBLANES, width), F32),
                      pltpu.VMEM((TM, width), F32)],
      compiler_params=pltpu.CompilerParams(
          dimension_semantics=("arbitrary", "arbitrary"),
          vmem_limit_bytes=VMEM_LIMIT_BYTES),
      name="mixer_d",
  )(x, g_pre.reshape(1, d), g_post.reshape(1, d), w1.astype(BF16),
    b1.reshape(1, -1),
    jnp.broadcast_to(conv_w[:, None, :], (D_CONV, SUBLANES, width)),
    conv_b.reshape(1, width), ln_g.reshape(1, width),
    ln_b.reshape(1, width), w2.astype(BF16), b2.reshape(1, d))


def kernel(x, norm_mix_pre, norm_mix_post, norm_ffn_pre, norm_ffn_post, a_w_in, a_w_s, a_b_s, a_ln_g, a_ln_b, a_w_out, b_w_in, b_w_grp, b_scale, b_w_out, c_w_in, c_conv_w, c_w_out, d_w1, d_b1, d_conv_w, d_conv_b, d_ln_g, d_ln_b, d_w2, d_b2, f_w_up, f_conv_w, f_w_down):
  depth = norm_mix_pre.shape[0]
  f_w_up_bf16 = _to_bf16(f_w_up)
  f_w_down_bf16 = _to_bf16(f_w_down)
  strips = False

  def want(flag, x, strips):
    if flag and not strips:
      return _to_strips(x), True
    if strips and not flag:
      return _from_strips(x), False
    return x, strips

  for i in range(depth):
    m, j = i % 4, i // 4
    gp, gq = norm_mix_pre[i], norm_mix_post[i]
    x, strips = want(m != 0, x, strips)
    if m == 0:
      x = _mixer_a_layer(x, gp, gq, a_w_in[j], a_w_s[j], a_b_s[j], a_ln_g[j],
                         a_ln_b[j], a_w_out[j])
      strips = True
    elif m == 1:
      x = _mixer_b_layer(x, gp, gq, b_w_in[j], b_w_grp[j], b_scale[j],
                         b_w_out[j])
    elif m == 2:
      x = _mixer_c_layer(x, gp, gq, c_w_in[j], c_conv_w[j], c_w_out[j])
    else:
      x = _mixer_d_layer(x, gp, gq, d_w1[j], d_b1[j], d_conv_w[j], d_conv_b[j],
                         d_ln_g[j], d_ln_b[j], d_w2[j], d_b2[j])
    x, strips = want(True, x, strips)
    x = _ffn_layer(x, i, norm_ffn_pre[i], norm_ffn_post[i], f_w_up_bf16,
                   f_conv_w, f_w_down_bf16)
  x, strips = want(False, x, strips)
  return x
```

```python
import functools

import jax
import jax.numpy as jnp
from jax import lax
from jax.experimental import pallas as pl
from jax.experimental.pallas import tpu as pltpu

EPS = 1e-6
CHUNK = 64
A_BLOCK = 128
A_HEADS = 8
B_WINDOWS = (2, 4, 8, 16)
B_GROUP_DIM = 256
D_CONV = 31

V7X_VMEM_BYTES = 64 * 2**20
VMEM_LIMIT_BYTES = V7X_VMEM_BYTES - 8 * 2**20
SUBLANES = 8
LANES = 128

TM = 512
STRIP = TM // SUBLANES
FFN_FC = 256

F32 = jnp.float32
BF16 = jnp.bfloat16


def _rms(x, g):
  ms = jnp.mean(x * x, axis=-1, keepdims=True)
  return x * lax.rsqrt(ms + EPS) * g


def _layernorm(x, g, b):
  mu = jnp.mean(x, axis=-1, keepdims=True)
  xc = x - mu
  var = jnp.mean(xc * xc, axis=-1, keepdims=True)
  return xc * lax.rsqrt(var + EPS) * g + b


def _dot(a, b):
  return jnp.dot(a, b, preferred_element_type=F32)


def _to_strips(x):
  b, s, d = x.shape
  return (x.reshape(b, s // TM, SUBLANES, STRIP, d).transpose(0, 1, 3, 2, 4)
          .reshape(b, s, d))


def _from_strips(x):
  b, s, d = x.shape
  return (x.reshape(b, s // TM, STRIP, SUBLANES, d).transpose(0, 1, 3, 2, 4)
          .reshape(b, s, d))


_CAST_BLOCK_BYTES = 8 * 2**20


def _cast_kernel(w_ref, o_ref):
  o_ref[...] = w_ref[...].astype(o_ref.dtype)


def _to_bf16(w):
  n, rows, cols = w.shape
  pack = 2 * SUBLANES
  fits = [r for r in range(pack, rows + 1, pack)
          if rows % r == 0 and r * cols * 4 <= _CAST_BLOCK_BYTES]
  rb = max(fits)
  spec = pl.BlockSpec((None, rb, cols), lambda l, r: (l, r, 0))
  return pl.pallas_call(
      _cast_kernel,
      grid=(n, rows // rb),
      in_specs=[spec],
      out_specs=spec,
      out_shape=jax.ShapeDtypeStruct(w.shape, BF16),
      compiler_params=pltpu.CompilerParams(
          dimension_semantics=("arbitrary", "arbitrary"),
          vmem_limit_bytes=VMEM_LIMIT_BYTES),
      name="to_bf16",
  )(w)


def _fill_halo(ext, prev, hr, cols=slice(None)):
  cur_rot = ext[TM - 1:TM - 1 + hr, cols]
  prev_rot = prev[SUBLANES - 1:SUBLANES - 1 + hr, cols]
  sub = lax.broadcasted_iota(jnp.int32, cur_rot.shape, 0) % SUBLANES
  ext[0:hr, cols] = jnp.where(sub == 0, prev_rot, cur_rot)
  prev[0:hr, cols] = ext[TM:TM + hr, cols]


_FFN_HR = 2 * SUBLANES


def _ffn_kernel(x_ref, gpre_ref, gpost_ref, wup_ref, cw_ref, wdn_ref, o_ref,
                h_scr, t_scr, prev_scr, *, fc, nc):
  hr = _FFN_HR
  d_ff = fc * nc

  @pl.when(pl.program_id(1) == 0)
  def _():
    prev_scr[...] = jnp.zeros_like(prev_scr)

  def cols(j):
    return (slice(j * fc, (j + 1) * fc),
            slice(d_ff + j * fc, d_ff + (j + 1) * fc))

  def up(j, slot):
    for half, cs in enumerate(cols(j)):
      t_scr[slot, hr:, half * fc:(half + 1) * fc] = _dot(h_scr[...],
                                                         wup_ref[:, cs])

  def proc(j, slot):
    ext = t_scr.at[slot]
    _fill_halo(ext, prev_scr.at[j], hr)
    halves = []
    for half, cs in enumerate(cols(j)):
      lanes = slice(half * fc, (half + 1) * fc)
      halves.append(cw_ref[2:3, cs] * ext[hr:hr + TM, lanes]
                    + cw_ref[1:2, cs] * ext[hr - SUBLANES:hr - SUBLANES + TM, lanes]
                    + cw_ref[0:1, cs] * ext[0:TM, lanes])
    g, u = halves
    a = (g * jax.nn.sigmoid(g) * u).astype(BF16)
    return _dot(a, wdn_ref[j * fc:(j + 1) * fc, :])

  x = x_ref[...]
  h_scr[...] = _rms(x, gpre_ref[...]).astype(BF16)
  up(0, 0)
  y = None
  for j in range(nc):
    if j + 1 < nc:
      up(j + 1, (j + 1) % 2)
    yj = proc(j, j % 2)
    y = yj if y is None else y + yj
  o_ref[...] = x + _rms(y, gpost_ref[...])


def _ffn_layer(x, layer, g_pre, g_post, w_up, conv_w, w_down):
  bsz, seq, d = x.shape
  d_ff = w_down.shape[1]
  fc = FFN_FC
  nc = d_ff // fc
  assert nc * fc == d_ff and seq % TM == 0
  kern = functools.partial(_ffn_kernel, fc=fc, nc=nc)
  full = lambda shape: pl.BlockSpec(shape, lambda b, s: (0,) * len(shape))
  of_layer = lambda shape: pl.BlockSpec(
      (None,) + shape, lambda b, s: (layer,) + (0,) * len(shape),
      pipeline_mode=pl.Buffered(1))
  return pl.pallas_call(
      kern,
      grid=(bsz, seq // TM),
      in_specs=[
          pl.BlockSpec((None, TM, d), lambda b, s: (b, s, 0)),
          full((1, d)), full((1, d)),
          of_layer((d, 2 * d_ff)), of_layer((3, 2 * d_ff)), of_layer((d_ff, d)),
      ],
      out_specs=pl.BlockSpec((None, TM, d), lambda b, s: (b, s, 0)),
      out_shape=jax.ShapeDtypeStruct(x.shape, x.dtype),
      scratch_shapes=[
          pltpu.VMEM((TM, d), BF16),
          pltpu.VMEM((2, TM + _FFN_HR, 2 * fc), F32),
          pltpu.VMEM((nc, _FFN_HR + SUBLANES, 2 * fc), F32),
      ],
      compiler_params=pltpu.CompilerParams(
          dimension_semantics=("arbitrary", "arbitrary"),
          vmem_limit_bytes=VMEM_LIMIT_BYTES),
      name="ffn",
  )(x, g_pre.reshape(1, d), g_post.reshape(1, d), w_up, conv_w, w_down)


def _mixer_a_kernel(x_ref, gpre_ref, gpost_ref, win_ref, ws_ref, bias_ref,
                    lng_ref, lnb_ref, wout_ref, o_ref, z_scr, *, width):
  q = lax.broadcasted_iota(jnp.int32, (A_BLOCK, A_BLOCK), 0) // CHUNK
  p = lax.broadcasted_iota(jnp.int32, (A_BLOCK, A_BLOCK), 1) // CHUNK
  mask = p <= q
  hd = width // A_HEADS
  wms = [jnp.where(mask, ws_ref[g], 0.0).astype(BF16) for g in range(A_HEADS)]

  def project(rows):
    x = x_ref[rows, :]
    h = _rms(x, gpre_ref[...]).astype(BF16)
    uv = _dot(h, win_ref[...])
    uv = 0.5 * uv * (1.0 + lax.erf(uv * (2.0 ** -0.5)))
    v = _layernorm(uv[:, width:], lng_ref[...], lnb_ref[...]).astype(BF16)
    return x, uv[:, :width], v

  def gate_and_mix(rows, x, u, v):
    for g in range(A_HEADS):
      cols = slice(g * hd, (g + 1) * hd)
      vv = jnp.concatenate([v[:A_BLOCK, cols], v[A_BLOCK:, cols]], axis=1)
      zz = _dot(wms[g], vv)
      bias = bias_ref[:, cols]
      z_scr[rows.start:rows.start + A_BLOCK, cols] = zz[:, :hd] + bias
      z_scr[rows.start + A_BLOCK:rows.stop, cols] = zz[:, hd:] + bias
    y = _dot((u * z_scr[rows, :]).astype(BF16), wout_ref[...])
    out = x + _rms(y, gpost_ref[...])
    for k in range((rows.stop - rows.start) // STRIP):
      s = rows.start // STRIP + k
      o_ref[:, s, :] = out[k * STRIP:(k + 1) * STRIP, :]

  pairs = [slice(r, r + 2 * A_BLOCK) for r in range(0, TM, 2 * A_BLOCK)]
  state = project(pairs[0])
  for i, rows in enumerate(pairs):
    nxt = project(pairs[i + 1]) if i + 1 < len(pairs) else None
    gate_and_mix(rows, *state)
    state = nxt


def _mixer_a_layer(x, g_pre, g_post, w_in, w_s, b_s, ln_g, ln_b, w_out):
  bsz, seq, d = x.shape
  width = w_out.shape[0]
  hd = width // A_HEADS
  bias = jnp.repeat(b_s.T, hd, axis=1)
  kern = functools.partial(_mixer_a_kernel, width=width)
  full = lambda shape: pl.BlockSpec(shape, lambda b, s: (0,) * len(shape))
  return pl.pallas_call(
      kern,
      grid=(bsz, seq // TM),
      in_specs=[
          pl.BlockSpec((None, TM, d), lambda b, s: (b, s, 0)),
          full((1, d)), full((1, d)),
          full((d, 2 * width)), full((A_HEADS, A_BLOCK, A_BLOCK)),
          full((A_BLOCK, width)), full((1, width)), full((1, width)),
          full((width, d)),
      ],
      out_specs=pl.BlockSpec((None, None, STRIP, SUBLANES, d),
                             lambda b, s: (b, s, 0, 0, 0)),
      out_shape=jax.ShapeDtypeStruct(
          (bsz, seq // TM, STRIP, SUBLANES, d), x.dtype),
      scratch_shapes=[pltpu.VMEM((TM, width), F32)],
      compiler_params=pltpu.CompilerParams(
          dimension_semantics=("arbitrary", "arbitrary"),
          vmem_limit_bytes=VMEM_LIMIT_BYTES),
      name="mixer_a",
  )(x, g_pre.reshape(1, d), g_post.reshape(1, d), w_in.astype(BF16), w_s, bias,
    ln_g.reshape(1, width), ln_b.reshape(1, width), w_out.astype(BF16)
    ).reshape(bsz, seq, d)


_B_HR = (max(B_WINDOWS) - 1) * SUBLANES


def _mixer_b_kernel(x_ref, gpre_ref, gpost_ref, win_ref, wgrp_ref, scale_ref,
                    wout_ref, o_ref, ext_scr, prev_scr, mix_scr):
  hr = _B_HR
  s = pl.program_id(1)

  @pl.when(s == 0)
  def _():
    prev_scr[...] = jnp.zeros_like(prev_scr)

  x = x_ref[...]
  h = _rms(x, gpre_ref[...]).astype(BF16)
  ext_scr[hr:, :] = _dot(h, win_ref[...])
  _fill_halo(ext_scr, prev_scr, hr)
  gd = B_GROUP_DIM
  row = lax.broadcasted_iota(jnp.int32, (TM, gd), 0)
  pos = s * TM + (row % SUBLANES) * STRIP + row // SUBLANES + 1
  for g, w in enumerate(B_WINDOWS):
    cols = slice(g * gd, (g + 1) * gd)
    p = ext_scr[hr:, cols]
    win = p
    for k in range(1, w):
      win = win + ext_scr[hr - k * SUBLANES:hr - k * SUBLANES + TM, cols]
    cnt = jnp.minimum(pos, w).astype(F32)
    pooled = (win / cnt - p).astype(BF16)
    mix_scr[:, cols] = _dot(pooled, wgrp_ref[g])
  y = _dot((mix_scr[...] * scale_ref[...]).astype(BF16), wout_ref[...])
  o_ref[...] = x + _rms(y, gpost_ref[...])


def _mixer_b_layer(x, g_pre, g_post, w_in, w_grp, scale, w_out):
  bsz, seq, d = x.shape
  width = w_out.shape[0]
  ng, gd, _ = w_grp.shape
  assert gd == B_GROUP_DIM and ng == len(B_WINDOWS)
  full = lambda shape: pl.BlockSpec(shape, lambda b, s: (0,) * len(shape))
  return pl.pallas_call(
      _mixer_b_kernel,
      grid=(bsz, seq // TM),
      in_specs=[
          pl.BlockSpec((None, TM, d), lambda b, s: (b, s, 0)),
          full((1, d)), full((1, d)),
          full((d, width)), full((ng, gd, gd)), full((1, width)),
          full((width, d)),
      ],
      out_specs=pl.BlockSpec((None, TM, d), lambda b, s: (b, s, 0)),
      out_shape=jax.ShapeDtypeStruct(x.shape, x.dtype),
      scratch_shapes=[pltpu.VMEM((TM + _B_HR, width), F32),
                      pltpu.VMEM((_B_HR + SUBLANES, width), F32),
                      pltpu.VMEM((TM, width), F32)],
      compiler_params=pltpu.CompilerParams(
          dimension_semantics=("arbitrary", "arbitrary"),
          vmem_limit_bytes=VMEM_LIMIT_BYTES),
      name="mixer_b",
  )(x, g_pre.reshape(1, d), g_post.reshape(1, d), w_in.astype(BF16),
    w_grp.astype(BF16), scale.reshape(1, width), w_out.astype(BF16))


_C_HR = 2 * SUBLANES


def _mixer_c_kernel(x_ref, gpre_ref, gpost_ref, win_ref, cw_ref, wout_ref,
                    o_ref, ext_scr, prev_scr, *, width):
  hr = _C_HR

  @pl.when(pl.program_id(1) == 0)
  def _():
    prev_scr[...] = jnp.zeros_like(prev_scr)

  x = x_ref[...]
  h = _rms(x, gpre_ref[...]).astype(BF16)
  proj = _dot(h, win_ref[...])
  bg = proj[:, :width]
  m = proj[:, width:2 * width] * proj[:, 2 * width:]
  ext_scr[hr:, :] = m
  _fill_halo(ext_scr, prev_scr, hr)
  cw = cw_ref[...]
  conv = (cw[2:3] * m
          + cw[1:2] * ext_scr[hr - SUBLANES:hr - SUBLANES + TM, :]
          + cw[0:1] * ext_scr[0:TM, :])
  y = _dot((bg * conv).astype(BF16), wout_ref[...])
  o_ref[...] = x + _rms(y, gpost_ref[...])


def _mixer_c_layer(x, g_pre, g_post, w_in, conv_w, w_out):
  bsz, seq, d = x.shape
  width = w_out.shape[0]
  kern = functools.partial(_mixer_c_kernel, width=width)
  full = lambda shape: pl.BlockSpec(shape, lambda b, s: (0,) * len(shape))
  return pl.pallas_call(
      kern,
      grid=(bsz, seq // TM),
      in_specs=[
          pl.BlockSpec((None, TM, d), lambda b, s: (b, s, 0)),
          full((1, d)), full((1, d)),
          full((d, 3 * width)), full((3, width)), full((width, d)),
      ],
      out_specs=pl.BlockSpec((None, TM, d), lambda b, s: (b, s, 0)),
      out_shape=jax.ShapeDtypeStruct(x.shape, x.dtype),
      scratch_shapes=[pltpu.VMEM((TM + _C_HR, width), F32),
                      pltpu.VMEM((_C_HR + SUBLANES, width), F32)],
      compiler_params=pltpu.CompilerParams(
          dimension_semantics=("arbitrary", "arbitrary"),
          vmem_limit_bytes=VMEM_LIMIT_BYTES),
      name="mixer_c",
  )(x, g_pre.reshape(1, d), g_post.reshape(1, d), w_in.astype(BF16), conv_w,
    w_out.astype(BF16))


_D_HR = (D_CONV - 1) * SUBLANES
_D_CB = 256
_D_ROWS = 16


def _mixer_d_kernel(x_ref, gpre_ref, gpost_ref, w1_ref, b1_ref, cw_ref, cb_ref,
                    lng_ref, lnb_ref, w2_ref, b2_ref, o_ref, ext_scr, prev_scr,
                    conv_scr, *, width):
  hr = _D_HR

  @pl.when(pl.program_id(1) == 0)
  def _():
    prev_scr[...] = jnp.zeros_like(prev_scr)

  x = x_ref[...]
  h = _rms(x, gpre_ref[...]).astype(BF16)

  order_bits = None
  for cb in range(width // _D_CB):
    cs = slice(cb * _D_CB, (cb + 1) * _D_CB)
    gs = slice(width + cb * _D_CB, width + (cb + 1) * _D_CB)
    a = _dot(h, w1_ref[:, cs]) + b1_ref[:, cs]
    gate = _dot(h, w1_ref[:, gs]) + b1_ref[:, gs]
    ext_scr[hr:, cs] = a * jax.nn.sigmoid(gate)
    _fill_halo(ext_scr, prev_scr, hr, cs)
    for lb in range(cb * _D_CB // LANES, (cb + 1) * _D_CB // LANES):
      cols = slice(lb * LANES, (lb + 1) * LANES)
      for i0 in range(0, STRIP, _D_ROWS):
        accs = [None] * _D_ROWS
        for k in range(D_CONV - 1, -1, -1):
          tap = cw_ref[k, :, cols]
          if order_bits is not None:
            tap = pltpu.bitcast(pltpu.bitcast(tap, jnp.uint32) | order_bits, F32)
          for g in range(_D_ROWS):
            off = (i0 + g + k) * SUBLANES
            term = tap * ext_scr[off:off + SUBLANES, cols]
            accs[g] = term if accs[g] is None else accs[g] + term
        for g in range(_D_ROWS):
          base = (i0 + g) * SUBLANES
          conv_scr[base:base + SUBLANES, cols] = accs[g]
        order_bits = (pltpu.bitcast(accs[0], jnp.uint32) >> 16) >> 16

  z = _layernorm(conv_scr[...] + cb_ref[...], lng_ref[...], lnb_ref[...])
  z = (z * jax.nn.sigmoid(z)).astype(BF16)
  y = _dot(z, w2_ref[...]) + b2_ref[...]
  o_ref[...] = x + _rms(y, gpost_ref[...])


def _mixer_d_layer(x, g_pre, g_post, w1, b1, conv_w, conv_b, ln_g, ln_b, w2,
                   b2):
  bsz, seq, d = x.shape
  width = w2.shape[0]
  assert conv_w.shape[0] == D_CONV and D_CONV - 1 <= STRIP
  kern = functools.partial(_mixer_d_kernel, width=width)
  full = lambda shape: pl.BlockSpec(shape, lambda b, s: (0,) * len(shape))
  return pl.pallas_call(
      kern,
      grid=(bsz, seq // TM),
      in_specs=[
          pl.BlockSpec((None, TM, d), lambda b, s: (b, s, 0)),
          full((1, d)), full((1, d)),
          full((d, 2 * width)), full((1, 2 * width)),
          full((D_CONV, SUBLANES, width)),
          full((1, width)), full((1, width)), full((1, width)),
          full((width, d)), full((1, d)),
      ],
      out_specs=pl.BlockSpec((None, TM, d), lambda b, s: (b, s, 0)),
      out_shape=jax.ShapeDtypeStruct(x.shape, x.dtype),
      scratch_shapes=[pltpu.VMEM((TM + _D_HR, width), F32),
                      pltpu.VMEM((_D_HR + SUBLANES, width), F32),
                      pltpu.VMEM((TM, width), F32)],
      compiler_params=pltpu.CompilerParams(
          dimension_semantics=("arbitrary", "arbitrary"),
          vmem_limit_bytes=VMEM_LIMIT_BYTES),
      name="mixer_d",
  )(x, g_pre.reshape(1, d), g_post.reshape(1, d), w1.astype(BF16),
    b1.reshape(1, -1),
    jnp.broadcast_to(conv_w[:, None, :], (D_CONV, SUBLANES, width)),
    conv_b.reshape(1, width), ln_g.reshape(1, width),
    ln_b.reshape(1, width), w2.astype(BF16), b2.reshape(1, d))


def kernel(x, norm_mix_pre, norm_mix_post, norm_ffn_pre, norm_ffn_post, a_w_in, a_w_s, a_b_s, a_ln_g, a_ln_b, a_w_out, b_w_in, b_w_grp, b_scale, b_w_out, c_w_in, c_conv_w, c_w_out, d_w1, d_b1, d_conv_w, d_conv_b, d_ln_g, d_ln_b, d_w2, d_b2, f_w_up, f_conv_w, f_w_down):
  depth = norm_mix_pre.shape[0]
  f_w_up_bf16 = _to_bf16(f_w_up)
  f_w_down_bf16 = _to_bf16(f_w_down)
  strips = False

  def want(flag, x, strips):
    if flag and not strips:
      return _to_strips(x), True
    if strips and not flag:
      return _from_strips(x), False
    return x, strips

  for i in range(depth):
    m, j = i % 4, i // 4
    gp, gq = norm_mix_pre[i], norm_mix_post[i]
    x, strips = want(m != 0, x, strips)
    if m == 0:
      x = _mixer_a_layer(x, gp, gq, a_w_in[j], a_w_s[j], a_b_s[j], a_ln_g[j],
                         a_ln_b[j], a_w_out[j])
      strips = True
    elif m == 1:
      x = _mixer_b_layer(x, gp, gq, b_w_in[j], b_w_grp[j], b_scale[j],
                         b_w_out[j])
    elif m == 2:
      x = _mixer_c_layer(x, gp, gq, c_w_in[j], c_conv_w[j], c_w_out[j])
    else:
      x = _mixer_d_layer(x, gp, gq, d_w1[j], d_b1[j], d_conv_w[j], d_conv_b[j],
                         d_ln_g[j], d_ln_b[j], d_w2[j], d_b2[j])
    x, strips = want(True, x, strips)
    x = _ffn_layer(x, i, norm_ffn_pre[i], norm_ffn_post[i], f_w_up_bf16,
                   f_conv_w, f_w_down_bf16)
  x, strips = want(False, x, strips)
  return x
```

```python
import functools

import jax
import jax.numpy as jnp
from jax import lax
from jax.experimental import pallas as pl
from jax.experimental.pallas import tpu as pltpu

EPS = 1e-6
CHUNK = 64
A_BLOCK = 128
A_HEADS = 8
B_WINDOWS = (2, 4, 8, 16)
B_GROUP_DIM = 256
D_CONV = 31

V7X_VMEM_BYTES = 64 * 2**20
VMEM_LIMIT_BYTES = V7X_VMEM_BYTES - 8 * 2**20
SUBLANES = 8
LANES = 128

TM = 512
STRIP = TM // SUBLANES
FFN_FC = 256

F32 = jnp.float32
BF16 = jnp.bfloat16


def _rms(x, g):
  ms = jnp.mean(x * x, axis=-1, keepdims=True)
  return x * lax.rsqrt(ms + EPS) * g


def _layernorm(x, g, b):
  mu = jnp.mean(x, axis=-1, keepdims=True)
  xc = x - mu
  var = jnp.mean(xc * xc, axis=-1, keepdims=True)
  return xc * lax.rsqrt(var + EPS) * g + b


def _dot(a, b):
  return jnp.dot(a, b, preferred_element_type=F32)


def _to_strips(x):
  b, s, d = x.shape
  return (x.reshape(b, s // TM, SUBLANES, STRIP, d).transpose(0, 1, 3, 2, 4)
          .reshape(b, s, d))


def _from_strips(x):
  b, s, d = x.shape
  return (x.reshape(b, s // TM, STRIP, SUBLANES, d).transpose(0, 1, 3, 2, 4)
          .reshape(b, s, d))


_CAST_BLOCK_BYTES = 8 * 2**20


def _cast_kernel(w_ref, o_ref):
  o_ref[...] = w_ref[...].astype(o_ref.dtype)


def _to_bf16(w):
  n, rows, cols = w.shape
  pack = 2 * SUBLANES
  fits = [r for r in range(pack, rows + 1, pack)
          if rows % r == 0 and r * cols * 4 <= _CAST_BLOCK_BYTES]
  rb = max(fits)
  spec = pl.BlockSpec((None, rb, cols), lambda l, r: (l, r, 0))
  return pl.pallas_call(
      _cast_kernel,
      grid=(n, rows // rb),
      in_specs=[spec],
      out_specs=spec,
      out_shape=jax.ShapeDtypeStruct(w.shape, BF16),
      compiler_params=pltpu.CompilerParams(
          dimension_semantics=("arbitrary", "arbitrary"),
          vmem_limit_bytes=VMEM_LIMIT_BYTES),
      name="to_bf16",
  )(w)


def _fill_halo(ext, prev, hr, cols=slice(None)):
  cur_rot = ext[TM - 1:TM - 1 + hr, cols]
  prev_rot = prev[SUBLANES - 1:SUBLANES - 1 + hr, cols]
  sub = lax.broadcasted_iota(jnp.int32, cur_rot.shape, 0) % SUBLANES
  ext[0:hr, cols] = jnp.where(sub == 0, prev_rot, cur_rot)
  prev[0:hr, cols] = ext[TM:TM + hr, cols]


_FFN_HR = 2 * SUBLANES


def _ffn_kernel(x_ref, gpre_ref, gpost_ref, wup_ref, cw_ref, wdn_ref, o_ref,
                h_scr, t_scr, prev_scr, *maybe_slab_scr, fc, nc,
                token_order_out):
  hr = _FFN_HR
  if token_order_out:
    slab_scr, = maybe_slab_scr
  d_ff = fc * nc

  @pl.when(pl.program_id(1) == 0)
  def _():
    prev_scr[...] = jnp.zeros_like(prev_scr)

  def cols(j):
    return (slice(j * fc, (j + 1) * fc),
            slice(d_ff + j * fc, d_ff + (j + 1) * fc))

  def up(j, slot):
    for half, cs in enumerate(cols(j)):
      t_scr[slot, hr:, half * fc:(half + 1) * fc] = _dot(h_scr[...],
                                                         wup_ref[:, cs])

  def proc(j, slot):
    ext = t_scr.at[slot]
    _fill_halo(ext, prev_scr.at[j], hr)
    halves = []
    for half, cs in enumerate(cols(j)):
      lanes = slice(half * fc, (half + 1) * fc)
      halves.append(cw_ref[2:3, cs] * ext[hr:hr + TM, lanes]
                    + cw_ref[1:2, cs] * ext[hr - SUBLANES:hr - SUBLANES + TM, lanes]
                    + cw_ref[0:1, cs] * ext[0:TM, lanes])
    g, u = halves
    a = (g * jax.nn.sigmoid(g) * u).astype(BF16)
    return _dot(a, wdn_ref[j * fc:(j + 1) * fc, :])

  x = x_ref[...]
  h_scr[...] = _rms(x, gpre_ref[...]).astype(BF16)
  up(0, 0)
  y = None
  for j in range(nc):
    if j + 1 < nc:
      up(j + 1, (j + 1) % 2)
    yj = proc(j, j % 2)
    y = yj if y is None else y + yj
  out = x + _rms(y, gpost_ref[...])
  if token_order_out:
    for lb in range(out.shape[-1] // LANES):
      lanes = slice(lb * LANES, (lb + 1) * LANES)
      slab_scr[lb] = out[:, lanes]
      for s in range(SUBLANES):
        o_ref[s, :, lanes] = slab_scr[lb, pl.ds(s, STRIP, stride=SUBLANES), :]
  else:
    o_ref[...] = out


def _ffn_layer(x, layer, g_pre, g_post, w_up, conv_w, w_down, token_order_out):
  bsz, seq, d = x.shape
  d_ff = w_down.shape[1]
  fc = FFN_FC
  nc = d_ff // fc
  assert nc * fc == d_ff and seq % TM == 0
  kern = functools.partial(_ffn_kernel, fc=fc, nc=nc,
                           token_order_out=token_order_out)
  if token_order_out:
    out_spec = pl.BlockSpec((None, None, SUBLANES, STRIP, d),
                            lambda b, s: (b, s, 0, 0, 0))
    out_shape = (bsz, seq // TM, SUBLANES, STRIP, d)
  else:
    out_spec = pl.BlockSpec((None, TM, d), lambda b, s: (b, s, 0))
    out_shape = x.shape
  full = lambda shape: pl.BlockSpec(shape, lambda b, s: (0,) * len(shape))
  of_layer = lambda shape: pl.BlockSpec(
      (None,) + shape, lambda b, s: (layer,) + (0,) * len(shape),
      pipeline_mode=pl.Buffered(1))
  return pl.pallas_call(
      kern,
      grid=(bsz, seq // TM),
      in_specs=[
          pl.BlockSpec((None, TM, d), lambda b, s: (b, s, 0)),
          full((1, d)), full((1, d)),
          of_layer((d, 2 * d_ff)), of_layer((3, 2 * d_ff)), of_layer((d_ff, d)),
      ],
      out_specs=out_spec,
      out_shape=jax.ShapeDtypeStruct(out_shape, x.dtype),
      scratch_shapes=[
          pltpu.VMEM((TM, d), BF16),
          pltpu.VMEM((2, TM + _FFN_HR, 2 * fc), F32),
          pltpu.VMEM((nc, _FFN_HR + SUBLANES, 2 * fc), F32),
      ] + ([pltpu.VMEM((d // LANES, TM, LANES), F32)] if token_order_out else []),
      compiler_params=pltpu.CompilerParams(
          dimension_semantics=("arbitrary", "arbitrary"),
          vmem_limit_bytes=VMEM_LIMIT_BYTES),
      name="ffn",
  )(x, g_pre.reshape(1, d), g_post.reshape(1, d), w_up, conv_w, w_down
    ).reshape(bsz, seq, d)


def _mixer_a_kernel(x_ref, gpre_ref, gpost_ref, win_ref, ws_ref, bias_ref,
                    lng_ref, lnb_ref, wout_ref, o_ref, z_scr, *, width):
  q = lax.broadcasted_iota(jnp.int32, (A_BLOCK, A_BLOCK), 0) // CHUNK
  p = lax.broadcasted_iota(jnp.int32, (A_BLOCK, A_BLOCK), 1) // CHUNK
  mask = p <= q
  hd = width // A_HEADS
  wms = [jnp.where(mask, ws_ref[g], 0.0).astype(BF16) for g in range(A_HEADS)]

  def project(rows):
    x = x_ref[rows, :]
    h = _rms(x, gpre_ref[...]).astype(BF16)
    uv = _dot(h, win_ref[...])
    uv = 0.5 * uv * (1.0 + lax.erf(uv * (2.0 ** -0.5)))
    v = _layernorm(uv[:, width:], lng_ref[...], lnb_ref[...]).astype(BF16)
    return x, uv[:, :width], v

  def gate_and_mix(rows, x, u, v):
    for g in range(A_HEADS):
      cols = slice(g * hd, (g + 1) * hd)
      vv = jnp.concatenate([v[:A_BLOCK, cols], v[A_BLOCK:, cols]], axis=1)
      zz = _dot(wms[g], vv)
      bias = bias_ref[:, cols]
      z_scr[rows.start:rows.start + A_BLOCK, cols] = zz[:, :hd] + bias
      z_scr[rows.start + A_BLOCK:rows.stop, cols] = zz[:, hd:] + bias
    y = _dot((u * z_scr[rows, :]).astype(BF16), wout_ref[...])
    o_ref[rows, :] = x + _rms(y, gpost_ref[...])

  pairs = [slice(r, r + 2 * A_BLOCK) for r in range(0, TM, 2 * A_BLOCK)]
  state = project(pairs[0])
  for i, rows in enumerate(pairs):
    nxt = project(pairs[i + 1]) if i + 1 < len(pairs) else None
    gate_and_mix(rows, *state)
    state = nxt


def _mixer_a_layer(x, g_pre, g_post, w_in, w_s, b_s, ln_g, ln_b, w_out):
  bsz, seq, d = x.shape
  width = w_out.shape[0]
  hd = width // A_HEADS
  bias = jnp.repeat(b_s.T, hd, axis=1)
  kern = functools.partial(_mixer_a_kernel, width=width)
  full = lambda shape: pl.BlockSpec(shape, lambda b, s: (0,) * len(shape))
  return pl.pallas_call(
      kern,
      grid=(bsz, seq // TM),
      in_specs=[
          pl.BlockSpec((None, TM, d), lambda b, s: (b, s, 0)),
          full((1, d)), full((1, d)),
          full((d, 2 * width)), full((A_HEADS, A_BLOCK, A_BLOCK)),
          full((A_BLOCK, width)), full((1, width)), full((1, width)),
          full((width, d)),
      ],
      out_specs=pl.BlockSpec((None, TM, d), lambda b, s: (b, s, 0)),
      out_shape=jax.ShapeDtypeStruct(x.shape, x.dtype),
      scratch_shapes=[pltpu.VMEM((TM, width), F32)],
      compiler_params=pltpu.CompilerParams(
          dimension_semantics=("arbitrary", "arbitrary"),
          vmem_limit_bytes=VMEM_LIMIT_BYTES),
      name="mixer_a",
  )(x, g_pre.reshape(1, d), g_post.reshape(1, d), w_in.astype(BF16), w_s, bias,
    ln_g.reshape(1, width), ln_b.reshape(1, width), w_out.astype(BF16))


_B_HR = (max(B_WINDOWS) - 1) * SUBLANES


def _mixer_b_kernel(x_ref, gpre_ref, gpost_ref, win_ref, wgrp_ref, scale_ref,
                    wout_ref, o_ref, ext_scr, prev_scr, mix_scr):
  hr = _B_HR
  s = pl.program_id(1)

  @pl.when(s == 0)
  def _():
    prev_scr[...] = jnp.zeros_like(prev_scr)

  x = x_ref[...]
  h = _rms(x, gpre_ref[...]).astype(BF16)
  ext_scr[hr:, :] = _dot(h, win_ref[...])
  _fill_halo(ext_scr, prev_scr, hr)
  gd = B_GROUP_DIM
  row = lax.broadcasted_iota(jnp.int32, (TM, gd), 0)
  pos = s * TM + (row % SUBLANES) * STRIP + row // SUBLANES + 1
  for g, w in enumerate(B_WINDOWS):
    cols = slice(g * gd, (g + 1) * gd)
    p = ext_scr[hr:, cols]
    win = p
    for k in range(1, w):
      win = win + ext_scr[hr - k * SUBLANES:hr - k * SUBLANES + TM, cols]
    cnt = jnp.minimum(pos, w).astype(F32)
    pooled = (win / cnt - p).astype(BF16)
    mix_scr[:, cols] = _dot(pooled, wgrp_ref[g])
  y = _dot((mix_scr[...] * scale_ref[...]).astype(BF16), wout_ref[...])
  o_ref[...] = x + _rms(y, gpost_ref[...])


def _mixer_b_layer(x, g_pre, g_post, w_in, w_grp, scale, w_out):
  bsz, seq, d = x.shape
  width = w_out.shape[0]
  ng, gd, _ = w_grp.shape
  assert gd == B_GROUP_DIM and ng == len(B_WINDOWS)
  full = lambda shape: pl.BlockSpec(shape, lambda b, s: (0,) * len(shape))
  return pl.pallas_call(
      _mixer_b_kernel,
      grid=(bsz, seq // TM),
      in_specs=[
          pl.BlockSpec((None, TM, d), lambda b, s: (b, s, 0)),
          full((1, d)), full((1, d)),
          full((d, width)), full((ng, gd, gd)), full((1, width)),
          full((width, d)),
      ],
      out_specs=pl.BlockSpec((None, TM, d), lambda b, s: (b, s, 0)),
      out_shape=jax.ShapeDtypeStruct(x.shape, x.dtype),
      scratch_shapes=[pltpu.VMEM((TM + _B_HR, width), F32),
                      pltpu.VMEM((_B_HR + SUBLANES, width), F32),
                      pltpu.VMEM((TM, width), F32)],
      compiler_params=pltpu.CompilerParams(
          dimension_semantics=("arbitrary", "arbitrary"),
          vmem_limit_bytes=VMEM_LIMIT_BYTES),
      name="mixer_b",
  )(x, g_pre.reshape(1, d), g_post.reshape(1, d), w_in.astype(BF16),
    w_grp.astype(BF16), scale.reshape(1, width), w_out.astype(BF16))


_C_HR = 2 * SUBLANES


def _mixer_c_kernel(x_ref, gpre_ref, gpost_ref, win_ref, cw_ref, wout_ref,
                    o_ref, ext_scr, prev_scr, *, width):
  hr = _C_HR

  @pl.when(pl.program_id(1) == 0)
  def _():
    prev_scr[...] = jnp.zeros_like(prev_scr)

  x = x_ref[...]
  h = _rms(x, gpre_ref[...]).astype(BF16)
  proj = _dot(h, win_ref[...])
  bg = proj[:, :width]
  m = proj[:, width:2 * width] * proj[:, 2 * width:]
  ext_scr[hr:, :] = m
  _fill_halo(ext_scr, prev_scr, hr)
  cw = cw_ref[...]
  conv = (cw[2:3] * m
          + cw[1:2] * ext_scr[hr - SUBLANES:hr - SUBLANES + TM, :]
          + cw[0:1] * ext_scr[0:TM, :])
  y = _dot((bg * conv).astype(BF16), wout_ref[...])
  o_ref[...] = x + _rms(y, gpost_ref[...])


def _mixer_c_layer(x, g_pre, g_post, w_in, conv_w, w_out):
  bsz, seq, d = x.shape
  width = w_out.shape[0]
  kern = functools.partial(_mixer_c_kernel, width=width)
  full = lambda shape: pl.BlockSpec(shape, lambda b, s: (0,) * len(shape))
  return pl.pallas_call(
      kern,
      grid=(bsz, seq // TM),
      in_specs=[
          pl.BlockSpec((None, TM, d), lambda b, s: (b, s, 0)),
          full((1, d)), full((1, d)),
          full((d, 3 * width)), full((3, width)), full((width, d)),
      ],
      out_specs=pl.BlockSpec((None, TM, d), lambda b, s: (b, s, 0)),
      out_shape=jax.ShapeDtypeStruct(x.shape, x.dtype),
      scratch_shapes=[pltpu.VMEM((TM + _C_HR, width), F32),
                      pltpu.VMEM((_C_HR + SUBLANES, width), F32)],
      compiler_params=pltpu.CompilerParams(
          dimension_semantics=("arbitrary", "arbitrary"),
          vmem_limit_bytes=VMEM_LIMIT_BYTES),
      name="mixer_c",
  )(x, g_pre.reshape(1, d), g_post.reshape(1, d), w_in.astype(BF16), conv_w,
    w_out.astype(BF16))


_D_HR = (D_CONV - 1) * SUBLANES
_D_CB = 256
_D_ROWS = 16


def _mixer_d_kernel(x_ref, gpre_ref, gpost_ref, w1_ref, b1_ref, cw_ref, cb_ref,
                    lng_ref, lnb_ref, w2_ref, b2_ref, o_ref, ext_scr, prev_scr,
                    conv_scr, *, width):
  hr = _D_HR

  @pl.when(pl.program_id(1) == 0)
  def _():
    prev_scr[...] = jnp.zeros_like(prev_scr)

  x = x_ref[...]
  h = _rms(x, gpre_ref[...]).astype(BF16)

  order_bits = None
  for cb in range(width // _D_CB):
    cs = slice(cb * _D_CB, (cb + 1) * _D_CB)
    gs = slice(width + cb * _D_CB, width + (cb + 1) * _D_CB)
    a = _dot(h, w1_ref[:, cs]) + b1_ref[:, cs]
    gate = _dot(h, w1_ref[:, gs]) + b1_ref[:, gs]
    ext_scr[hr:, cs] = a * jax.nn.sigmoid(gate)
    _fill_halo(ext_scr, prev_scr, hr, cs)
    for lb in range(cb * _D_CB // LANES, (cb + 1) * _D_CB // LANES):
      cols = slice(lb * LANES, (lb + 1) * LANES)
      for i0 in range(0, STRIP, _D_ROWS):
        accs = [None] * _D_ROWS
        for k in range(D_CONV - 1, -1, -1):
          tap = cw_ref[k, :, cols]
          if order_bits is not None:
            tap = pltpu.bitcast(pltpu.bitcast(tap, jnp.uint32) | order_bits, F32)
          for g in range(_D_ROWS):
            off = (i0 + g + k) * SUBLANES
            term = tap * ext_scr[off:off + SUBLANES, cols]
            accs[g] = term if accs[g] is None else accs[g] + term
        for g in range(_D_ROWS):
          base = (i0 + g) * SUBLANES
          conv_scr[base:base + SUBLANES, cols] = accs[g]
        order_bits = (pltpu.bitcast(accs[0], jnp.uint32) >> 16) >> 16

  z = _layernorm(conv_scr[...] + cb_ref[...], lng_ref[...], lnb_ref[...])
  z = (z * jax.nn.sigmoid(z)).astype(BF16)
  y = _dot(z, w2_ref[...]) + b2_ref[...]
  o_ref[...] = x + _rms(y, gpost_ref[...])


def _mixer_d_layer(x, g_pre, g_post, w1, b1, conv_w, conv_b, ln_g, ln_b, w2,
                   b2):
  bsz, seq, d = x.shape
  width = w2.shape[0]
  assert conv_w.shape[0] == D_CONV and D_CONV - 1 <= STRIP
  kern = functools.partial(_mixer_d_kernel, width=width)
  full = lambda shape: pl.BlockSpec(shape, lambda b, s: (0,) * len(shape))
  return pl.pallas_call(
      kern,
      grid=(bsz, seq // TM),
      in_specs=[
          pl.BlockSpec((None, TM, d), lambda b, s: (b, s, 0)),
          full((1, d)), full((1, d)),
          full((d, 2 * width)), full((1, 2 * width)),
          full((D_CONV, SUBLANES, width)),
          full((1, width)), full((1, width)), full((1, width)),
          full((width, d)), full((1, d)),
      ],
      out_specs=pl.BlockSpec((None, TM, d), lambda b, s: (b, s, 0)),
      out_shape=jax.ShapeDtypeStruct(x.shape, x.dtype),
      scratch_shapes=[pltpu.VMEM((TM + _D_HR, width), F32),
                      pltpu.VMEM((_D_HR + SUBLANES, width), F32),
                      pltpu.VMEM((TM, width), F32)],
      compiler_params=pltpu.CompilerParams(
          dimension_semantics=("arbitrary", "arbitrary"),
          vmem_limit_bytes=VMEM_LIMIT_BYTES),
      name="mixer_d",
  )(x, g_pre.reshape(1, d), g_post.reshape(1, d), w1.astype(BF16),
    b1.reshape(1, -1),
    jnp.broadcast_to(conv_w[:, None, :], (D_CONV, SUBLANES, width)),
    conv_b.reshape(1, width), ln_g.reshape(1, width),
    ln_b.reshape(1, width), w2.astype(BF16), b2.reshape(1, d))


def kernel(x, norm_mix_pre, norm_mix_post, norm_ffn_pre, norm_ffn_post, a_w_in, a_w_s, a_b_s, a_ln_g, a_ln_b, a_w_out, b_w_in, b_w_grp, b_scale, b_w_out, c_w_in, c_conv_w, c_w_out, d_w1, d_b1, d_conv_w, d_conv_b, d_ln_g, d_ln_b, d_w2, d_b2, f_w_up, f_conv_w, f_w_down):
  depth = norm_mix_pre.shape[0]
  f_w_up_bf16 = _to_bf16(f_w_up)
  f_w_down_bf16 = _to_bf16(f_w_down)
  strips = False

  def want(flag, x, strips):
    if flag and not strips:
      return _to_strips(x), True
    if strips and not flag:
      return _from_strips(x), False
    return x, strips

  for i in range(depth):
    m, j = i % 4, i // 4
    gp, gq = norm_mix_pre[i], norm_mix_post[i]
    x, strips = want(m != 0, x, strips)
    if m == 0:
      x = _mixer_a_layer(x, gp, gq, a_w_in[j], a_w_s[j], a_b_s[j], a_ln_g[j],
                         a_ln_b[j], a_w_out[j])
    elif m == 1:
      x = _mixer_b_layer(x, gp, gq, b_w_in[j], b_w_grp[j], b_scale[j],
                         b_w_out[j])
    elif m == 2:
      x = _mixer_c_layer(x, gp, gq, c_w_in[j], c_conv_w[j], c_w_out[j])
    else:
      x = _mixer_d_layer(x, gp, gq, d_w1[j], d_b1[j], d_conv_w[j], d_conv_b[j],
                         d_ln_g[j], d_ln_b[j], d_w2[j], d_b2[j])
    x, strips = want(True, x, strips)
    last = i + 1 == depth or (i + 1) % 4 == 0
    x = _ffn_layer(x, i, norm_ffn_pre[i], norm_ffn_post[i], f_w_up_bf16,
                   f_conv_w, f_w_down_bf16, token_order_out=last)
    strips = not last
  x, strips = want(False, x, strips)
  return x
```

```python
import functools

import jax
import jax.numpy as jnp
from jax import lax
from jax.experimental import pallas as pl
from jax.experimental.pallas import tpu as pltpu

EPS = 1e-6
CHUNK = 64
A_BLOCK = 128
A_HEADS = 8
B_WINDOWS = (2, 4, 8, 16)
B_GROUP_DIM = 256
D_CONV = 31

V7X_VMEM_BYTES = 64 * 2**20
VMEM_LIMIT_BYTES = V7X_VMEM_BYTES - 8 * 2**20
SUBLANES = 8
LANES = 128

TM = 512
STRIP = TM // SUBLANES
FFN_FC = 256

F32 = jnp.float32
BF16 = jnp.bfloat16


def _rms(x, g):
  ms = jnp.mean(x * x, axis=-1, keepdims=True)
  return x * lax.rsqrt(ms + EPS) * g


def _layernorm(x, g, b):
  mu = jnp.mean(x, axis=-1, keepdims=True)
  xc = x - mu
  var = jnp.mean(xc * xc, axis=-1, keepdims=True)
  return xc * lax.rsqrt(var + EPS) * g + b


def _dot(a, b):
  return jnp.dot(a, b, preferred_element_type=F32)


def _to_strips(x):
  b, s, d = x.shape
  return (x.reshape(b, s // TM, SUBLANES, STRIP, d).transpose(0, 1, 3, 2, 4)
          .reshape(b, s, d))


def _from_strips(x):
  b, s, d = x.shape
  return (x.reshape(b, s // TM, STRIP, SUBLANES, d).transpose(0, 1, 3, 2, 4)
          .reshape(b, s, d))


_CAST_BLOCK_BYTES = 8 * 2**20


def _cast_kernel(w_ref, o_ref):
  o_ref[...] = w_ref[...].astype(o_ref.dtype)


def _to_bf16(w):
  n, rows, cols = w.shape
  pack = 2 * SUBLANES
  fits = [r for r in range(pack, rows + 1, pack)
          if rows % r == 0 and r * cols * 4 <= _CAST_BLOCK_BYTES]
  rb = max(fits)
  spec = pl.BlockSpec((None, rb, cols), lambda l, r: (l, r, 0))
  return pl.pallas_call(
      _cast_kernel,
      grid=(n, rows // rb),
      in_specs=[spec],
      out_specs=spec,
      out_shape=jax.ShapeDtypeStruct(w.shape, BF16),
      compiler_params=pltpu.CompilerParams(
          dimension_semantics=("arbitrary", "arbitrary"),
          vmem_limit_bytes=VMEM_LIMIT_BYTES),
      name="to_bf16",
  )(w)


def _fill_halo(ext, prev, hr, cols=slice(None)):
  cur_rot = ext[TM - 1:TM - 1 + hr, cols]
  prev_rot = prev[SUBLANES - 1:SUBLANES - 1 + hr, cols]
  sub = lax.broadcasted_iota(jnp.int32, cur_rot.shape, 0) % SUBLANES
  ext[0:hr, cols] = jnp.where(sub == 0, prev_rot, cur_rot)
  prev[0:hr, cols] = ext[TM:TM + hr, cols]


_FFN_HR = 2 * SUBLANES


def _ffn_kernel(x_ref, gpre_ref, gpost_ref, wup_ref, cw_ref, wdn_ref, o_ref,
                h_scr, t_scr, prev_scr, slab_scr, *, fc, nc, token_order_in,
                token_order_out):
  hr = _FFN_HR
  d_ff = fc * nc
  n_slabs = slab_scr.shape[0]

  @pl.when(pl.program_id(1) == 0)
  def _():
    prev_scr[...] = jnp.zeros_like(prev_scr)

  def cols(j):
    return (slice(j * fc, (j + 1) * fc),
            slice(d_ff + j * fc, d_ff + (j + 1) * fc))

  def up(j, slot):
    for half, cs in enumerate(cols(j)):
      t_scr[slot, hr:, half * fc:(half + 1) * fc] = _dot(h_scr[...],
                                                         wup_ref[:, cs])

  def proc(j, slot):
    ext = t_scr.at[slot]
    _fill_halo(ext, prev_scr.at[j], hr)
    halves = []
    for half, cs in enumerate(cols(j)):
      lanes = slice(half * fc, (half + 1) * fc)
      halves.append(cw_ref[2:3, cs] * ext[hr:hr + TM, lanes]
                    + cw_ref[1:2, cs] * ext[hr - SUBLANES:hr - SUBLANES + TM, lanes]
                    + cw_ref[0:1, cs] * ext[0:TM, lanes])
    g, u = halves
    a = (g * jax.nn.sigmoid(g) * u).astype(BF16)
    return _dot(a, wdn_ref[j * fc:(j + 1) * fc, :])

  if token_order_in:
    for lb in range(n_slabs):
      lanes = slice(lb * LANES, (lb + 1) * LANES)
      for s in range(SUBLANES):
        slab_scr[lb, pl.ds(s, STRIP, stride=SUBLANES), :] = x_ref[s, :, lanes]
    x = jnp.concatenate([slab_scr[lb] for lb in range(n_slabs)], axis=1)
  else:
    x = x_ref[...]
  h_scr[...] = _rms(x, gpre_ref[...]).astype(BF16)
  up(0, 0)
  y = None
  for j in range(nc):
    if j + 1 < nc:
      up(j + 1, (j + 1) % 2)
    yj = proc(j, j % 2)
    y = yj if y is None else y + yj
  out = x + _rms(y, gpost_ref[...])
  if token_order_out:
    for lb in range(n_slabs):
      lanes = slice(lb * LANES, (lb + 1) * LANES)
      slab_scr[lb] = out[:, lanes]
      for s in range(SUBLANES):
        o_ref[s, :, lanes] = slab_scr[lb, pl.ds(s, STRIP, stride=SUBLANES), :]
  else:
    o_ref[...] = out


def _ffn_layer(x, layer, g_pre, g_post, w_up, conv_w, w_down, token_order_in,
               token_order_out):
  bsz, seq, d = x.shape
  d_ff = w_down.shape[1]
  fc = FFN_FC
  nc = d_ff // fc
  assert nc * fc == d_ff and seq % TM == 0
  kern = functools.partial(_ffn_kernel, fc=fc, nc=nc,
                           token_order_in=token_order_in,
                           token_order_out=token_order_out)
  strip_spec = pl.BlockSpec((None, TM, d), lambda b, s: (b, s, 0))
  token_spec = pl.BlockSpec((None, None, SUBLANES, STRIP, d),
                            lambda b, s: (b, s, 0, 0, 0))
  token_shape = (bsz, seq // TM, SUBLANES, STRIP, d)
  if token_order_in:
    x = x.reshape(token_shape)
  out_spec = token_spec if token_order_out else strip_spec
  out_shape = token_shape if token_order_out else (bsz, seq, d)
  full = lambda shape: pl.BlockSpec(shape, lambda b, s: (0,) * len(shape))
  of_layer = lambda shape: pl.BlockSpec(
      (None,) + shape, lambda b, s: (layer,) + (0,) * len(shape),
      pipeline_mode=pl.Buffered(1))
  return pl.pallas_call(
      kern,
      grid=(bsz, seq // TM),
      in_specs=[
          token_spec if token_order_in else strip_spec,
          full((1, d)), full((1, d)),
          of_layer((d, 2 * d_ff)), of_layer((3, 2 * d_ff)), of_layer((d_ff, d)),
      ],
      out_specs=out_spec,
      out_shape=jax.ShapeDtypeStruct(out_shape, x.dtype),
      scratch_shapes=[
          pltpu.VMEM((TM, d), BF16),
          pltpu.VMEM((2, TM + _FFN_HR, 2 * fc), F32),
          pltpu.VMEM((nc, _FFN_HR + SUBLANES, 2 * fc), F32),
          pltpu.VMEM((d // LANES, TM, LANES), F32),
      ],
      compiler_params=pltpu.CompilerParams(
          dimension_semantics=("arbitrary", "arbitrary"),
          vmem_limit_bytes=VMEM_LIMIT_BYTES),
      name="ffn",
  )(x, g_pre.reshape(1, d), g_post.reshape(1, d), w_up, conv_w, w_down
    ).reshape(bsz, seq, d)


def _mixer_a_kernel(x_ref, gpre_ref, gpost_ref, win_ref, ws_ref, bias_ref,
                    lng_ref, lnb_ref, wout_ref, o_ref, z_scr, *, width):
  q = lax.broadcasted_iota(jnp.int32, (A_BLOCK, A_BLOCK), 0) // CHUNK
  p = lax.broadcasted_iota(jnp.int32, (A_BLOCK, A_BLOCK), 1) // CHUNK
  mask = p <= q
  hd = width // A_HEADS
  wms = [jnp.where(mask, ws_ref[g], 0.0).astype(BF16) for g in range(A_HEADS)]

  def project(rows):
    x = x_ref[rows, :]
    h = _rms(x, gpre_ref[...]).astype(BF16)
    uv = _dot(h, win_ref[...])
    uv = 0.5 * uv * (1.0 + lax.erf(uv * (2.0 ** -0.5)))
    v = _layernorm(uv[:, width:], lng_ref[...], lnb_ref[...]).astype(BF16)
    return x, uv[:, :width], v

  def gate_and_mix(rows, x, u, v):
    for g in range(A_HEADS):
      cols = slice(g * hd, (g + 1) * hd)
      vv = jnp.concatenate([v[:A_BLOCK, cols], v[A_BLOCK:, cols]], axis=1)
      zz = _dot(wms[g], vv)
      bias = bias_ref[:, cols]
      z_scr[rows.start:rows.start + A_BLOCK, cols] = zz[:, :hd] + bias
      z_scr[rows.start + A_BLOCK:rows.stop, cols] = zz[:, hd:] + bias
    y = _dot((u * z_scr[rows, :]).astype(BF16), wout_ref[...])
    o_ref[rows, :] = x + _rms(y, gpost_ref[...])

  pairs = [slice(r, r + 2 * A_BLOCK) for r in range(0, TM, 2 * A_BLOCK)]
  state = project(pairs[0])
  for i, rows in enumerate(pairs):
    nxt = project(pairs[i + 1]) if i + 1 < len(pairs) else None
    gate_and_mix(rows, *state)
    state = nxt


def _mixer_a_layer(x, g_pre, g_post, w_in, w_s, b_s, ln_g, ln_b, w_out):
  bsz, seq, d = x.shape
  width = w_out.shape[0]
  hd = width // A_HEADS
  bias = jnp.repeat(b_s.T, hd, axis=1)
  kern = functools.partial(_mixer_a_kernel, width=width)
  full = lambda shape: pl.BlockSpec(shape, lambda b, s: (0,) * len(shape))
  return pl.pallas_call(
      kern,
      grid=(bsz, seq // TM),
      in_specs=[
          pl.BlockSpec((None, TM, d), lambda b, s: (b, s, 0)),
          full((1, d)), full((1, d)),
          full((d, 2 * width)), full((A_HEADS, A_BLOCK, A_BLOCK)),
          full((A_BLOCK, width)), full((1, width)), full((1, width)),
          full((width, d)),
      ],
      out_specs=pl.BlockSpec((None, TM, d), lambda b, s: (b, s, 0)),
      out_shape=jax.ShapeDtypeStruct(x.shape, x.dtype),
      scratch_shapes=[pltpu.VMEM((TM, width), F32)],
      compiler_params=pltpu.CompilerParams(
          dimension_semantics=("arbitrary", "arbitrary"),
          vmem_limit_bytes=VMEM_LIMIT_BYTES),
      name="mixer_a",
  )(x, g_pre.reshape(1, d), g_post.reshape(1, d), w_in.astype(BF16), w_s, bias,
    ln_g.reshape(1, width), ln_b.reshape(1, width), w_out.astype(BF16))


_B_HR = (max(B_WINDOWS) - 1) * SUBLANES


def _mixer_b_kernel(x_ref, gpre_ref, gpost_ref, win_ref, wgrp_ref, scale_ref,
                    wout_ref, o_ref, ext_scr, prev_scr, mix_scr):
  hr = _B_HR
  s = pl.program_id(1)

  @pl.when(s == 0)
  def _():
    prev_scr[...] = jnp.zeros_like(prev_scr)

  x = x_ref[...]
  h = _rms(x, gpre_ref[...]).astype(BF16)
  ext_scr[hr:, :] = _dot(h, win_ref[...])
  _fill_halo(ext_scr, prev_scr, hr)
  gd = B_GROUP_DIM
  row = lax.broadcasted_iota(jnp.int32, (TM, gd), 0)
  pos = s * TM + (row % SUBLANES) * STRIP + row // SUBLANES + 1
  for g, w in enumerate(B_WINDOWS):
    cols = slice(g * gd, (g + 1) * gd)
    p = ext_scr[hr:, cols]
    win = p
    for k in range(1, w):
      win = win + ext_scr[hr - k * SUBLANES:hr - k * SUBLANES + TM, cols]
    cnt = jnp.minimum(pos, w).astype(F32)
    pooled = (win / cnt - p).astype(BF16)
    mix_scr[:, cols] = _dot(pooled, wgrp_ref[g])
  y = _dot((mix_scr[...] * scale_ref[...]).astype(BF16), wout_ref[...])
  o_ref[...] = x + _rms(y, gpost_ref[...])


def _mixer_b_layer(x, g_pre, g_post, w_in, w_grp, scale, w_out):
  bsz, seq, d = x.shape
  width = w_out.shape[0]
  ng, gd, _ = w_grp.shape
  assert gd == B_GROUP_DIM and ng == len(B_WINDOWS)
  full = lambda shape: pl.BlockSpec(shape, lambda b, s: (0,) * len(shape))
  return pl.pallas_call(
      _mixer_b_kernel,
      grid=(bsz, seq // TM),
      in_specs=[
          pl.BlockSpec((None, TM, d), lambda b, s: (b, s, 0)),
          full((1, d)), full((1, d)),
          full((d, width)), full((ng, gd, gd)), full((1, width)),
          full((width, d)),
      ],
      out_specs=pl.BlockSpec((None, TM, d), lambda b, s: (b, s, 0)),
      out_shape=jax.ShapeDtypeStruct(x.shape, x.dtype),
      scratch_shapes=[pltpu.VMEM((TM + _B_HR, width), F32),
                      pltpu.VMEM((_B_HR + SUBLANES, width), F32),
                      pltpu.VMEM((TM, width), F32)],
      compiler_params=pltpu.CompilerParams(
          dimension_semantics=("arbitrary", "arbitrary"),
          vmem_limit_bytes=VMEM_LIMIT_BYTES),
      name="mixer_b",
  )(x, g_pre.reshape(1, d), g_post.reshape(1, d), w_in.astype(BF16),
    w_grp.astype(BF16), scale.reshape(1, width), w_out.astype(BF16))


_C_HR = 2 * SUBLANES


def _mixer_c_kernel(x_ref, gpre_ref, gpost_ref, win_ref, cw_ref, wout_ref,
                    o_ref, ext_scr, prev_scr, *, width):
  hr = _C_HR

  @pl.when(pl.program_id(1) == 0)
  def _():
    prev_scr[...] = jnp.zeros_like(prev_scr)

  x = x_ref[...]
  h = _rms(x, gpre_ref[...]).astype(BF16)
  proj = _dot(h, win_ref[...])
  bg = proj[:, :width]
  m = proj[:, width:2 * width] * proj[:, 2 * width:]
  ext_scr[hr:, :] = m
  _fill_halo(ext_scr, prev_scr, hr)
  cw = cw_ref[...]
  conv = (cw[2:3] * m
          + cw[1:2] * ext_scr[hr - SUBLANES:hr - SUBLANES + TM, :]
          + cw[0:1] * ext_scr[0:TM, :])
  y = _dot((bg * conv).astype(BF16), wout_ref[...])
  o_ref[...] = x + _rms(y, gpost_ref[...])


def _mixer_c_layer(x, g_pre, g_post, w_in, conv_w, w_out):
  bsz, seq, d = x.shape
  width = w_out.shape[0]
  kern = functools.partial(_mixer_c_kernel, width=width)
  full = lambda shape: pl.BlockSpec(shape, lambda b, s: (0,) * len(shape))
  return pl.pallas_call(
      kern,
      grid=(bsz, seq // TM),
      in_specs=[
          pl.BlockSpec((None, TM, d), lambda b, s: (b, s, 0)),
          full((1, d)), full((1, d)),
          full((d, 3 * width)), full((3, width)), full((width, d)),
      ],
      out_specs=pl.BlockSpec((None, TM, d), lambda b, s: (b, s, 0)),
      out_shape=jax.ShapeDtypeStruct(x.shape, x.dtype),
      scratch_shapes=[pltpu.VMEM((TM + _C_HR, width), F32),
                      pltpu.VMEM((_C_HR + SUBLANES, width), F32)],
      compiler_params=pltpu.CompilerParams(
          dimension_semantics=("arbitrary", "arbitrary"),
          vmem_limit_bytes=VMEM_LIMIT_BYTES),
      name="mixer_c",
  )(x, g_pre.reshape(1, d), g_post.reshape(1, d), w_in.astype(BF16), conv_w,
    w_out.astype(BF16))


_D_HR = (D_CONV - 1) * SUBLANES
_D_CB = 256
_D_ROWS = 16


def _mixer_d_kernel(x_ref, gpre_ref, gpost_ref, w1_ref, b1_ref, cw_ref, cb_ref,
                    lng_ref, lnb_ref, w2_ref, b2_ref, o_ref, ext_scr, prev_scr,
                    conv_scr, *, width):
  hr = _D_HR

  @pl.when(pl.program_id(1) == 0)
  def _():
    prev_scr[...] = jnp.zeros_like(prev_scr)

  x = x_ref[...]
  h = _rms(x, gpre_ref[...]).astype(BF16)

  order_bits = None
  for cb in range(width // _D_CB):
    cs = slice(cb * _D_CB, (cb + 1) * _D_CB)
    gs = slice(width + cb * _D_CB, width + (cb + 1) * _D_CB)
    a = _dot(h, w1_ref[:, cs]) + b1_ref[:, cs]
    gate = _dot(h, w1_ref[:, gs]) + b1_ref[:, gs]
    ext_scr[hr:, cs] = a * jax.nn.sigmoid(gate)
    _fill_halo(ext_scr, prev_scr, hr, cs)
    for lb in range(cb * _D_CB // LANES, (cb + 1) * _D_CB // LANES):
      cols = slice(lb * LANES, (lb + 1) * LANES)
      for i0 in range(0, STRIP, _D_ROWS):
        accs = [None] * _D_ROWS
        for k in range(D_CONV - 1, -1, -1):
          tap = cw_ref[k, :, cols]
          if order_bits is not None:
            tap = pltpu.bitcast(pltpu.bitcast(tap, jnp.uint32) | order_bits, F32)
          for g in range(_D_ROWS):
            off = (i0 + g + k) * SUBLANES
            term = tap * ext_scr[off:off + SUBLANES, cols]
            accs[g] = term if accs[g] is None else accs[g] + term
        for g in range(_D_ROWS):
          base = (i0 + g) * SUBLANES
          conv_scr[base:base + SUBLANES, cols] = accs[g]
        order_bits = (pltpu.bitcast(accs[0], jnp.uint32) >> 16) >> 16

  z = _layernorm(conv_scr[...] + cb_ref[...], lng_ref[...], lnb_ref[...])
  z = (z * jax.nn.sigmoid(z)).astype(BF16)
  y = _dot(z, w2_ref[...]) + b2_ref[...]
  o_ref[...] = x + _rms(y, gpost_ref[...])


def _mixer_d_layer(x, g_pre, g_post, w1, b1, conv_w, conv_b, ln_g, ln_b, w2,
                   b2):
  bsz, seq, d = x.shape
  width = w2.shape[0]
  assert conv_w.shape[0] == D_CONV and D_CONV - 1 <= STRIP
  kern = functools.partial(_mixer_d_kernel, width=width)
  full = lambda shape: pl.BlockSpec(shape, lambda b, s: (0,) * len(shape))
  return pl.pallas_call(
      kern,
      grid=(bsz, seq // TM),
      in_specs=[
          pl.BlockSpec((None, TM, d), lambda b, s: (b, s, 0)),
          full((1, d)), full((1, d)),
          full((d, 2 * width)), full((1, 2 * width)),
          full((D_CONV, SUBLANES, width)),
          full((1, width)), full((1, width)), full((1, width)),
          full((width, d)), full((1, d)),
      ],
      out_specs=pl.BlockSpec((None, TM, d), lambda b, s: (b, s, 0)),
      out_shape=jax.ShapeDtypeStruct(x.shape, x.dtype),
      scratch_shapes=[pltpu.VMEM((TM + _D_HR, width), F32),
                      pltpu.VMEM((_D_HR + SUBLANES, width), F32),
                      pltpu.VMEM((TM, width), F32)],
      compiler_params=pltpu.CompilerParams(
          dimension_semantics=("arbitrary", "arbitrary"),
          vmem_limit_bytes=VMEM_LIMIT_BYTES),
      name="mixer_d",
  )(x, g_pre.reshape(1, d), g_post.reshape(1, d), w1.astype(BF16),
    b1.reshape(1, -1),
    jnp.broadcast_to(conv_w[:, None, :], (D_CONV, SUBLANES, width)),
    conv_b.reshape(1, width), ln_g.reshape(1, width),
    ln_b.reshape(1, width), w2.astype(BF16), b2.reshape(1, d))


def kernel(x, norm_mix_pre, norm_mix_post, norm_ffn_pre, norm_ffn_post, a_w_in, a_w_s, a_b_s, a_ln_g, a_ln_b, a_w_out, b_w_in, b_w_grp, b_scale, b_w_out, c_w_in, c_conv_w, c_w_out, d_w1, d_b1, d_conv_w, d_conv_b, d_ln_g, d_ln_b, d_w2, d_b2, f_w_up, f_conv_w, f_w_down):
  depth = norm_mix_pre.shape[0]
  f_w_up_bf16 = _to_bf16(f_w_up)
  f_w_down_bf16 = _to_bf16(f_w_down)
  strips = False

  def want(flag, x, strips):
    if flag and not strips:
      return _to_strips(x), True
    if strips and not flag:
      return _from_strips(x), False
    return x, strips

  for i in range(depth):
    m, j = i % 4, i // 4
    gp, gq = norm_mix_pre[i], norm_mix_post[i]
    x, strips = want(m != 0, x, strips)
    if m == 0:
      x = _mixer_a_layer(x, gp, gq, a_w_in[j], a_w_s[j], a_b_s[j], a_ln_g[j],
                         a_ln_b[j], a_w_out[j])
    elif m == 1:
      x = _mixer_b_layer(x, gp, gq, b_w_in[j], b_w_grp[j], b_scale[j],
                         b_w_out[j])
    elif m == 2:
      x = _mixer_c_layer(x, gp, gq, c_w_in[j], c_conv_w[j], c_w_out[j])
    else:
      x = _mixer_d_layer(x, gp, gq, d_w1[j], d_b1[j], d_conv_w[j], d_conv_b[j],
                         d_ln_g[j], d_ln_b[j], d_w2[j], d_b2[j])
    last = i + 1 == depth or (i + 1) % 4 == 0
    x = _ffn_layer(x, i, norm_ffn_pre[i], norm_ffn_post[i], f_w_up_bf16,
                   f_conv_w, f_w_down_bf16, token_order_in=not strips,
                   token_order_out=last)
    strips = not last
  x, strips = want(False, x, strips)
  return x
```

```python
import functools

import jax
import jax.numpy as jnp
from jax import lax
from jax.experimental import pallas as pl
from jax.experimental.pallas import tpu as pltpu

EPS = 1e-6
CHUNK = 64
A_BLOCK = 128
A_HEADS = 8
B_WINDOWS = (2, 4, 8, 16)
B_GROUP_DIM = 256
D_CONV = 31

V7X_VMEM_BYTES = 64 * 2**20
VMEM_LIMIT_BYTES = V7X_VMEM_BYTES - 8 * 2**20
SUBLANES = 8
LANES = 128

TM = 512
STRIP = TM // SUBLANES
FFN_FC = 256

F32 = jnp.float32
BF16 = jnp.bfloat16


def _rms(x, g):
  ms = jnp.mean(x * x, axis=-1, keepdims=True)
  return x * lax.rsqrt(ms + EPS) * g


def _layernorm(x, g, b):
  mu = jnp.mean(x, axis=-1, keepdims=True)
  xc = x - mu
  var = jnp.mean(xc * xc, axis=-1, keepdims=True)
  return xc * lax.rsqrt(var + EPS) * g + b


def _dot(a, b):
  return jnp.dot(a, b, preferred_element_type=F32)


def _to_strips(x):
  b, s, d = x.shape
  return (x.reshape(b, s // TM, SUBLANES, STRIP, d).transpose(0, 1, 3, 2, 4)
          .reshape(b, s, d))


def _from_strips(x):
  b, s, d = x.shape
  return (x.reshape(b, s // TM, STRIP, SUBLANES, d).transpose(0, 1, 3, 2, 4)
          .reshape(b, s, d))


_CAST_BLOCK_BYTES = 8 * 2**20


def _cast_kernel(w_ref, o_ref):
  o_ref[...] = w_ref[...].astype(o_ref.dtype)


def _to_bf16(w):
  n, rows, cols = w.shape
  pack = 2 * SUBLANES
  fits = [r for r in range(pack, rows + 1, pack)
          if rows % r == 0 and r * cols * 4 <= _CAST_BLOCK_BYTES]
  rb = max(fits)
  spec = pl.BlockSpec((None, rb, cols), lambda l, r: (l, r, 0))
  return pl.pallas_call(
      _cast_kernel,
      grid=(n, rows // rb),
      in_specs=[spec],
      out_specs=spec,
      out_shape=jax.ShapeDtypeStruct(w.shape, BF16),
      compiler_params=pltpu.CompilerParams(
          dimension_semantics=("arbitrary", "arbitrary"),
          vmem_limit_bytes=VMEM_LIMIT_BYTES),
      name="to_bf16",
  )(w)


def _fill_halo(ext, prev, hr, cols=slice(None)):
  cur_rot = ext[TM - 1:TM - 1 + hr, cols]
  prev_rot = prev[SUBLANES - 1:SUBLANES - 1 + hr, cols]
  sub = lax.broadcasted_iota(jnp.int32, cur_rot.shape, 0) % SUBLANES
  ext[0:hr, cols] = jnp.where(sub == 0, prev_rot, cur_rot)
  prev[0:hr, cols] = ext[TM:TM + hr, cols]


_FFN_HR = 2 * SUBLANES


def _ffn_kernel(x_ref, gpre_ref, gpost_ref, wup_ref, cw_ref, wdn_ref, o_ref,
                h_scr, t_scr, prev_scr, slab_scr, *, fc, nc, token_order_in,
                token_order_out):
  hr = _FFN_HR
  d_ff = fc * nc
  n_slabs = slab_scr.shape[0]

  @pl.when(pl.program_id(1) == 0)
  def _():
    prev_scr[...] = jnp.zeros_like(prev_scr)

  def cols(j):
    return (slice(j * fc, (j + 1) * fc),
            slice(d_ff + j * fc, d_ff + (j + 1) * fc))

  def up(j, slot):
    for half, cs in enumerate(cols(j)):
      t_scr[slot, hr:, half * fc:(half + 1) * fc] = _dot(h_scr[...],
                                                         wup_ref[:, cs])

  def proc(j, slot):
    ext = t_scr.at[slot]
    _fill_halo(ext, prev_scr.at[j], hr)
    halves = []
    for half, cs in enumerate(cols(j)):
      lanes = slice(half * fc, (half + 1) * fc)
      halves.append(cw_ref[2:3, cs] * ext[hr:hr + TM, lanes]
                    + cw_ref[1:2, cs] * ext[hr - SUBLANES:hr - SUBLANES + TM, lanes]
                    + cw_ref[0:1, cs] * ext[0:TM, lanes])
    g, u = halves
    a = (g * jax.nn.sigmoid(g) * u).astype(BF16)
    return _dot(a, wdn_ref[j * fc:(j + 1) * fc, :])

  if token_order_in:
    for lb in range(n_slabs):
      lanes = slice(lb * LANES, (lb + 1) * LANES)
      for s in range(SUBLANES):
        slab_scr[lb, pl.ds(s, STRIP, stride=SUBLANES), :] = x_ref[s, :, lanes]
    x = jnp.concatenate([slab_scr[lb] for lb in range(n_slabs)], axis=1)
  else:
    x = x_ref[...]
  h_scr[...] = _rms(x, gpre_ref[...]).astype(BF16)
  up(0, 0)
  y = None
  for j in range(nc):
    if j + 1 < nc:
      up(j + 1, (j + 1) % 2)
    yj = proc(j, j % 2)
    y = yj if y is None else y + yj
  out = x + _rms(y, gpost_ref[...])
  if token_order_out:
    for lb in range(n_slabs):
      lanes = slice(lb * LANES, (lb + 1) * LANES)
      slab_scr[lb] = out[:, lanes]
      for s in range(SUBLANES):
        o_ref[s, :, lanes] = slab_scr[lb, pl.ds(s, STRIP, stride=SUBLANES), :]
  else:
    o_ref[...] = out


def _ffn_layer(x, layer, g_pre, g_post, w_up, conv_w, w_down, token_order_in,
               token_order_out):
  bsz, seq, d = x.shape
  d_ff = w_down.shape[1]
  fc = FFN_FC
  nc = d_ff // fc
  assert nc * fc == d_ff and seq % TM == 0
  kern = functools.partial(_ffn_kernel, fc=fc, nc=nc,
                           token_order_in=token_order_in,
                           token_order_out=token_order_out)
  strip_spec = pl.BlockSpec((None, TM, d), lambda b, s: (b, s, 0))
  token_spec = pl.BlockSpec((None, None, SUBLANES, STRIP, d),
                            lambda b, s: (b, s, 0, 0, 0))
  token_shape = (bsz, seq // TM, SUBLANES, STRIP, d)
  if token_order_in:
    x = x.reshape(token_shape)
  out_spec = token_spec if token_order_out else strip_spec
  out_shape = token_shape if token_order_out else (bsz, seq, d)
  full = lambda shape: pl.BlockSpec(shape, lambda b, s: (0,) * len(shape))
  of_layer = lambda shape: pl.BlockSpec(
      (None,) + shape, lambda b, s: (layer,) + (0,) * len(shape),
      pipeline_mode=pl.Buffered(1))
  return pl.pallas_call(
      kern,
      grid=(bsz, seq // TM),
      in_specs=[
          token_spec if token_order_in else strip_spec,
          full((1, d)), full((1, d)),
          of_layer((d, 2 * d_ff)), of_layer((3, 2 * d_ff)), of_layer((d_ff, d)),
      ],
      out_specs=out_spec,
      out_shape=jax.ShapeDtypeStruct(out_shape, x.dtype),
      scratch_shapes=[
          pltpu.VMEM((TM, d), BF16),
          pltpu.VMEM((2, TM + _FFN_HR, 2 * fc), F32),
          pltpu.VMEM((nc, _FFN_HR + SUBLANES, 2 * fc), F32),
          pltpu.VMEM((d // LANES, TM, LANES), F32),
      ],
      compiler_params=pltpu.CompilerParams(
          dimension_semantics=("arbitrary", "arbitrary"),
          vmem_limit_bytes=VMEM_LIMIT_BYTES),
      name="ffn",
  )(x, g_pre.reshape(1, d), g_post.reshape(1, d), w_up, conv_w, w_down
    ).reshape(bsz, seq, d)


def _mixer_a_kernel(x_ref, gpre_ref, gpost_ref, win_ref, ws_ref, bias_ref,
                    lng_ref, lnb_ref, wout_ref, o_ref, z_scr, *, width):
  q = lax.broadcasted_iota(jnp.int32, (A_BLOCK, A_BLOCK), 0) // CHUNK
  p = lax.broadcasted_iota(jnp.int32, (A_BLOCK, A_BLOCK), 1) // CHUNK
  mask = p <= q
  hd = width // A_HEADS
  wms = [jnp.where(mask, ws_ref[g], 0.0).astype(BF16) for g in range(A_HEADS)]

  def project(rows):
    x = x_ref[rows, :]
    h = _rms(x, gpre_ref[...]).astype(BF16)
    uv = _dot(h, win_ref[...])
    uv = 0.5 * uv * (1.0 + lax.erf(uv * (2.0 ** -0.5)))
    v = _layernorm(uv[:, width:], lng_ref[...], lnb_ref[...]).astype(BF16)
    return x, uv[:, :width], v

  def gate_and_mix(rows, x, u, v):
    for g in range(A_HEADS):
      cols = slice(g * hd, (g + 1) * hd)
      vv = jnp.concatenate([v[:A_BLOCK, cols], v[A_BLOCK:, cols]], axis=1)
      zz = _dot(wms[g], vv)
      bias = bias_ref[:, cols]
      z_scr[rows.start:rows.start + A_BLOCK, cols] = zz[:, :hd] + bias
      z_scr[rows.start + A_BLOCK:rows.stop, cols] = zz[:, hd:] + bias
    y = _dot((u * z_scr[rows, :]).astype(BF16), wout_ref[...])
    o_ref[rows, :] = x + _rms(y, gpost_ref[...])

  pairs = [slice(r, r + 2 * A_BLOCK) for r in range(0, TM, 2 * A_BLOCK)]
  state = project(pairs[0])
  for i, rows in enumerate(pairs):
    nxt = project(pairs[i + 1]) if i + 1 < len(pairs) else None
    gate_and_mix(rows, *state)
    state = nxt


def _mixer_a_layer(x, g_pre, g_post, w_in, w_s, b_s, ln_g, ln_b, w_out):
  bsz, seq, d = x.shape
  width = w_out.shape[0]
  hd = width // A_HEADS
  bias = jnp.repeat(b_s.T, hd, axis=1)
  kern = functools.partial(_mixer_a_kernel, width=width)
  full = lambda shape: pl.BlockSpec(shape, lambda b, s: (0,) * len(shape))
  return pl.pallas_call(
      kern,
      grid=(bsz, seq // TM),
      in_specs=[
          pl.BlockSpec((None, TM, d), lambda b, s: (b, s, 0)),
          full((1, d)), full((1, d)),
          full((d, 2 * width)), full((A_HEADS, A_BLOCK, A_BLOCK)),
          full((A_BLOCK, width)), full((1, width)), full((1, width)),
          full((width, d)),
      ],
      out_specs=pl.BlockSpec((None, TM, d), lambda b, s: (b, s, 0)),
      out_shape=jax.ShapeDtypeStruct(x.shape, x.dtype),
      scratch_shapes=[pltpu.VMEM((TM, width), F32)],
      compiler_params=pltpu.CompilerParams(
          dimension_semantics=("arbitrary", "arbitrary"),
          vmem_limit_bytes=VMEM_LIMIT_BYTES),
      name="mixer_a",
  )(x, g_pre.reshape(1, d), g_post.reshape(1, d), w_in.astype(BF16), w_s, bias,
    ln_g.reshape(1, width), ln_b.reshape(1, width), w_out.astype(BF16))


_B_HR = (max(B_WINDOWS) - 1) * SUBLANES


def _mixer_b_kernel(x_ref, gpre_ref, gpost_ref, win_ref, wgrp_ref, scale_ref,
                    wout_ref, o_ref, ext_scr, prev_scr, mix_scr):
  hr = _B_HR
  s = pl.program_id(1)

  @pl.when(s == 0)
  def _():
    prev_scr[...] = jnp.zeros_like(prev_scr)

  x = x_ref[...]
  h = _rms(x, gpre_ref[...]).astype(BF16)
  ext_scr[hr:, :] = _dot(h, win_ref[...])
  _fill_halo(ext_scr, prev_scr, hr)
  gd = B_GROUP_DIM
  row = lax.broadcasted_iota(jnp.int32, (TM, gd), 0)
  pos = s * TM + (row % SUBLANES) * STRIP + row // SUBLANES + 1
  for g, w in enumerate(B_WINDOWS):
    cols = slice(g * gd, (g + 1) * gd)
    p = ext_scr[hr:, cols]
    win = p
    for k in range(1, w):
      win = win + ext_scr[hr - k * SUBLANES:hr - k * SUBLANES + TM, cols]
    cnt = jnp.minimum(pos, w).astype(F32)
    pooled = (win / cnt - p).astype(BF16)
    mix_scr[:, cols] = _dot(pooled, wgrp_ref[g])
  y = _dot((mix_scr[...] * scale_ref[...]).astype(BF16), wout_ref[...])
  o_ref[...] = x + _rms(y, gpost_ref[...])


def _mixer_b_layer(x, g_pre, g_post, w_in, w_grp, scale, w_out):
  bsz, seq, d = x.shape
  width = w_out.shape[0]
  ng, gd, _ = w_grp.shape
  assert gd == B_GROUP_DIM and ng == len(B_WINDOWS)
  full = lambda shape: pl.BlockSpec(shape, lambda b, s: (0,) * len(shape))
  return pl.pallas_call(
      _mixer_b_kernel,
      grid=(bsz, seq // TM),
      in_specs=[
          pl.BlockSpec((None, TM, d), lambda b, s: (b, s, 0)),
          full((1, d)), full((1, d)),
          full((d, width)), full((ng, gd, gd)), full((1, width)),
          full((width, d)),
      ],
      out_specs=pl.BlockSpec((None, TM, d), lambda b, s: (b, s, 0)),
      out_shape=jax.ShapeDtypeStruct(x.shape, x.dtype),
      scratch_shapes=[pltpu.VMEM((TM + _B_HR, width), F32),
                      pltpu.VMEM((_B_HR + SUBLANES, width), F32),
                      pltpu.VMEM((TM, width), F32)],
      compiler_params=pltpu.CompilerParams(
          dimension_semantics=("arbitrary", "arbitrary"),
          vmem_limit_bytes=VMEM_LIMIT_BYTES),
      name="mixer_b",
  )(x, g_pre.reshape(1, d), g_post.reshape(1, d), w_in.astype(BF16),
    w_grp.astype(BF16), scale.reshape(1, width), w_out.astype(BF16))


_C_HR = 2 * SUBLANES


def _mixer_c_kernel(x_ref, gpre_ref, gpost_ref, win_ref, cw_ref, wout_ref,
                    o_ref, ext_scr, prev_scr, *, width):
  hr = _C_HR

  @pl.when(pl.program_id(1) == 0)
  def _():
    prev_scr[...] = jnp.zeros_like(prev_scr)

  x = x_ref[...]
  h = _rms(x, gpre_ref[...]).astype(BF16)
  proj = _dot(h, win_ref[...])
  bg = proj[:, :width]
  m = proj[:, width:2 * width] * proj[:, 2 * width:]
  ext_scr[hr:, :] = m
  _fill_halo(ext_scr, prev_scr, hr)
  cw = cw_ref[...]
  conv = (cw[2:3] * m
          + cw[1:2] * ext_scr[hr - SUBLANES:hr - SUBLANES + TM, :]
          + cw[0:1] * ext_scr[0:TM, :])
  y = _dot((bg * conv).astype(BF16), wout_ref[...])
  o_ref[...] = x + _rms(y, gpost_ref[...])


def _mixer_c_layer(x, g_pre, g_post, w_in, conv_w, w_out):
  bsz, seq, d = x.shape
  width = w_out.shape[0]
  kern = functools.partial(_mixer_c_kernel, width=width)
  full = lambda shape: pl.BlockSpec(shape, lambda b, s: (0,) * len(shape))
  return pl.pallas_call(
      kern,
      grid=(bsz, seq // TM),
      in_specs=[
          pl.BlockSpec((None, TM, d), lambda b, s: (b, s, 0)),
          full((1, d)), full((1, d)),
          full((d, 3 * width)), full((3, width)), full((width, d)),
      ],
      out_specs=pl.BlockSpec((None, TM, d), lambda b, s: (b, s, 0)),
      out_shape=jax.ShapeDtypeStruct(x.shape, x.dtype),
      scratch_shapes=[pltpu.VMEM((TM + _C_HR, width), F32),
                      pltpu.VMEM((_C_HR + SUBLANES, width), F32)],
      compiler_params=pltpu.CompilerParams(
          dimension_semantics=("arbitrary", "arbitrary"),
          vmem_limit_bytes=VMEM_LIMIT_BYTES),
      name="mixer_c",
  )(x, g_pre.reshape(1, d), g_post.reshape(1, d), w_in.astype(BF16), conv_w,
    w_out.astype(BF16))


_D_HR = (D_CONV - 1) * SUBLANES
_D_CB = 128
_D_ROWS = 16


def _mixer_d_kernel(x_ref, gpre_ref, gpost_ref, w1_ref, b1_ref, cw_ref, cb_ref,
                    lng_ref, lnb_ref, w2_ref, b2_ref, o_ref, ext_scr, prev_scr,
                    conv_scr, *, width):
  hr = _D_HR

  @pl.when(pl.program_id(1) == 0)
  def _():
    prev_scr[...] = jnp.zeros_like(prev_scr)

  x = x_ref[...]
  h = _rms(x, gpre_ref[...]).astype(BF16)

  order_bits = None
  for cb in range(width // _D_CB):
    cs = slice(cb * _D_CB, (cb + 1) * _D_CB)
    gs = slice(width + cb * _D_CB, width + (cb + 1) * _D_CB)
    a = _dot(h, w1_ref[:, cs]) + b1_ref[:, cs]
    gate = _dot(h, w1_ref[:, gs]) + b1_ref[:, gs]
    ext_scr[hr:, cs] = a * jax.nn.sigmoid(gate)
    _fill_halo(ext_scr, prev_scr, hr, cs)
    for lb in range(cb * _D_CB // LANES, (cb + 1) * _D_CB // LANES):
      cols = slice(lb * LANES, (lb + 1) * LANES)
      for i0 in range(0, STRIP, _D_ROWS):
        accs = [None] * _D_ROWS
        for k in range(D_CONV - 1, -1, -1):
          tap = cw_ref[k, :, cols]
          if order_bits is not None:
            tap = pltpu.bitcast(pltpu.bitcast(tap, jnp.uint32) | order_bits, F32)
          for g in range(_D_ROWS):
            off = (i0 + g + k) * SUBLANES
            term = tap * ext_scr[off:off + SUBLANES, cols]
            accs[g] = term if accs[g] is None else accs[g] + term
        for g in range(_D_ROWS):
          base = (i0 + g) * SUBLANES
          conv_scr[base:base + SUBLANES, cols] = accs[g]
        order_bits = (pltpu.bitcast(accs[0], jnp.uint32) >> 16) >> 16

  z = _layernorm(conv_scr[...] + cb_ref[...], lng_ref[...], lnb_ref[...])
  z = (z * jax.nn.sigmoid(z)).astype(BF16)
  y = _dot(z, w2_ref[...]) + b2_ref[...]
  o_ref[...] = x + _rms(y, gpost_ref[...])


def _mixer_d_layer(x, g_pre, g_post, w1, b1, conv_w, conv_b, ln_g, ln_b, w2,
                   b2):
  bsz, seq, d = x.shape
  width = w2.shape[0]
  assert conv_w.shape[0] == D_CONV and D_CONV - 1 <= STRIP
  kern = functools.partial(_mixer_d_kernel, width=width)
  full = lambda shape: pl.BlockSpec(shape, lambda b, s: (0,) * len(shape))
  return pl.pallas_call(
      kern,
      grid=(bsz, seq // TM),
      in_specs=[
          pl.BlockSpec((None, TM, d), lambda b, s: (b, s, 0)),
          full((1, d)), full((1, d)),
          full((d, 2 * width)), full((1, 2 * width)),
          full((D_CONV, SUBLANES, width)),
          full((1, width)), full((1, width)), full((1, width)),
          full((width, d)), full((1, d)),
      ],
      out_specs=pl.BlockSpec((None, TM, d), lambda b, s: (b, s, 0)),
      out_shape=jax.ShapeDtypeStruct(x.shape, x.dtype),
      scratch_shapes=[pltpu.VMEM((TM + _D_HR, width), F32),
                      pltpu.VMEM((_D_HR + SUBLANES, width), F32),
                      pltpu.VMEM((TM, width), F32)],
      compiler_params=pltpu.CompilerParams(
          dimension_semantics=("arbitrary", "arbitrary"),
          vmem_limit_bytes=VMEM_LIMIT_BYTES),
      name="mixer_d",
  )(x, g_pre.reshape(1, d), g_post.reshape(1, d), w1.astype(BF16),
    b1.reshape(1, -1),
    jnp.broadcast_to(conv_w[:, None, :], (D_CONV, SUBLANES, width)),
    conv_b.reshape(1, width), ln_g.reshape(1, width),
    ln_b.reshape(1, width), w2.astype(BF16), b2.reshape(1, d))


def kernel(x, norm_mix_pre, norm_mix_post, norm_ffn_pre, norm_ffn_post, a_w_in, a_w_s, a_b_s, a_ln_g, a_ln_b, a_w_out, b_w_in, b_w_grp, b_scale, b_w_out, c_w_in, c_conv_w, c_w_out, d_w1, d_b1, d_conv_w, d_conv_b, d_ln_g, d_ln_b, d_w2, d_b2, f_w_up, f_conv_w, f_w_down):
  depth = norm_mix_pre.shape[0]
  f_w_up_bf16 = _to_bf16(f_w_up)
  f_w_down_bf16 = _to_bf16(f_w_down)
  strips = False

  def want(flag, x, strips):
    if flag and not strips:
      return _to_strips(x), True
    if strips and not flag:
      return _from_strips(x), False
    return x, strips

  for i in range(depth):
    m, j = i % 4, i // 4
    gp, gq = norm_mix_pre[i], norm_mix_post[i]
    x, strips = want(m != 0, x, strips)
    if m == 0:
      x = _mixer_a_layer(x, gp, gq, a_w_in[j], a_w_s[j], a_b_s[j], a_ln_g[j],
                         a_ln_b[j], a_w_out[j])
    elif m == 1:
      x = _mixer_b_layer(x, gp, gq, b_w_in[j], b_w_grp[j], b_scale[j],
                         b_w_out[j])
    elif m == 2:
      x = _mixer_c_layer(x, gp, gq, c_w_in[j], c_conv_w[j], c_w_out[j])
    else:
      x = _mixer_d_layer(x, gp, gq, d_w1[j], d_b1[j], d_conv_w[j], d_conv_b[j],
                         d_ln_g[j], d_ln_b[j], d_w2[j], d_b2[j])
    last = i + 1 == depth or (i + 1) % 4 == 0
    x = _ffn_layer(x, i, norm_ffn_pre[i], norm_ffn_post[i], f_w_up_bf16,
                   f_conv_w, f_w_down_bf16, token_order_in=not strips,
                   token_order_out=last)
    strips = not last
  x, strips = want(False, x, strips)
  return x
```
